```python
import math
import jax, jax.numpy as jnp
from jax import lax
import numpy as np

D_MODEL = 1024
BATCH = 8
SEQ = 2048
DEPTH = 4

N_A = DEPTH // 2
N_B = DEPTH - N_A
PLE_DIM = 256
N_HEADS = 16
HEAD_DIM = 64
ROPE_DIM = 32
QK_DIM = HEAD_DIM + ROPE_DIM
Q_LORA = 384
KV_LORA = 256
MIX_WIDTH = N_HEADS * HEAD_DIM
MLA_IN = Q_LORA + KV_LORA + ROPE_DIM + MIX_WIDTH
SB_IN = 2 * MIX_WIDTH
ROPE_THETA = 10000.0
Q_BLOCK = 128
EPS = 1e-6

kernel_name = "yoco_mla_stickbreaking_hybrid"


def rmsnorm(x, g):
    xf = x.astype(jnp.float32)
    y = xf * lax.rsqrt(jnp.mean(xf * xf, axis=-1, keepdims=True) + EPS)
    return (y * g.astype(jnp.float32)).astype(x.dtype)


def rope(x, pos):
    half = ROPE_DIM // 2
    inv = 1.0 / (ROPE_THETA ** (jnp.arange(half, dtype=jnp.float32) / half))
    ang = pos.astype(jnp.float32)[..., None] * inv
    cos = jnp.cos(ang)[:, :, None, :]
    sin = jnp.sin(ang)[:, :, None, :]
    x1 = x[..., :half].astype(jnp.float32)
    x2 = x[..., half:].astype(jnp.float32)
    out = jnp.concatenate([x1 * cos - x2 * sin, x2 * cos + x1 * sin], axis=-1)
    return out.astype(x.dtype)


def _blocks(q):
    B, H, S, d = q.shape
    nb = S // Q_BLOCK
    return q.reshape(B, H, nb, Q_BLOCK, d).transpose(2, 0, 1, 3, 4), nb


def _unblocks(o):
    nb, B, H, qb, d = o.shape
    return o.transpose(1, 2, 0, 3, 4).reshape(B, H, nb * qb, d)


def causal_softmax_attention(q, k, v):
    S = q.shape[2]
    qb, nb = _blocks(q)
    s_idx = jnp.arange(S)
    scale = QK_DIM ** -0.5

    def one(args):
        qi, bi = args
        t_idx = bi * Q_BLOCK + jnp.arange(Q_BLOCK)
        logits = jnp.einsum('bhqd,bhkd->bhqk', qi, k,
                            preferred_element_type=jnp.float32) * scale
        mask = s_idx[None, :] <= t_idx[:, None]
        w = jax.nn.softmax(jnp.where(mask, logits, -jnp.inf), axis=-1)
        return jnp.einsum('bhqk,bhkd->bhqd', w.astype(v.dtype), v)

    return _unblocks(lax.map(one, (qb, jnp.arange(nb))))


def stick_breaking_attention(q, k, v):
    S = q.shape[2]
    qb, nb = _blocks(q)
    s_idx = jnp.arange(S)
    scale = HEAD_DIM ** -0.5

    def one(args):
        qi, bi = args
        t_idx = bi * Q_BLOCK + jnp.arange(Q_BLOCK)
        z = jnp.einsum('bhqd,bhkd->bhqk', qi, k,
                       preferred_element_type=jnp.float32) * scale
        mask = s_idx[None, :] < t_idx[:, None]
        log_beta = jax.nn.log_sigmoid(z)
        log_one_minus = jnp.where(mask, jax.nn.log_sigmoid(-z), 0.0)
        tail = lax.cumsum(log_one_minus, axis=log_one_minus.ndim - 1,
                          reverse=True) - log_one_minus
        a = jnp.where(mask, jnp.exp(log_beta + tail), 0.0)
        return jnp.einsum('bhqk,bhkd->bhqd', a.astype(v.dtype), v)

    return _unblocks(lax.map(one, (qb, jnp.arange(nb))))


def mla_layer(x, positions, ln_g, w_in, q_norm_g, kv_norm_g, w_q_up, w_kv_up,
              q_head_g, k_head_g, w_out):
    B, S, _ = x.shape
    h = rmsnorm(x, ln_g)
    proj = h @ w_in
    c_q, c_kv, k_rope, gate = jnp.split(
        proj, [Q_LORA, Q_LORA + KV_LORA, Q_LORA + KV_LORA + ROPE_DIM], axis=-1)
    q = (rmsnorm(c_q, q_norm_g) @ w_q_up).reshape(B, S, N_HEADS, QK_DIM)
    kv = (rmsnorm(c_kv, kv_norm_g) @ w_kv_up).reshape(B, S, N_HEADS, 2 * HEAD_DIM)
    k_nope, v = kv[..., :HEAD_DIM], kv[..., HEAD_DIM:]
    k = jnp.concatenate(
        [k_nope, jnp.broadcast_to(k_rope[:, :, None, :], (B, S, N_HEADS, ROPE_DIM))], axis=-1)
    q = rmsnorm(q, q_head_g)
    k = rmsnorm(k, k_head_g)
    q = jnp.concatenate([q[..., :HEAD_DIM], rope(q[..., HEAD_DIM:], positions)], axis=-1)
    k = jnp.concatenate([k[..., :HEAD_DIM], rope(k[..., HEAD_DIM:], positions)], axis=-1)
    o = causal_softmax_attention(q.transpose(0, 2, 1, 3), k.transpose(0, 2, 1, 3),
                                 v.transpose(0, 2, 1, 3))
    o = o.transpose(0, 2, 1, 3).reshape(B, S, MIX_WIDTH) * jax.nn.silu(gate)
    return x + o @ w_out


def sb_layer(x, k_sh, v_sh, ln_g, w_in, w_out):
    B, S, _ = x.shape
    h = rmsnorm(x, ln_g)
    q, gate = jnp.split(h @ w_in, [MIX_WIDTH], axis=-1)
    q = q.reshape(B, S, N_HEADS, HEAD_DIM).transpose(0, 2, 1, 3)
    o = stick_breaking_attention(q, k_sh, v_sh)
    o = o.transpose(0, 2, 1, 3).reshape(B, S, MIX_WIDTH) * jax.nn.silu(gate)
    return x + o @ w_out


def setup_inputs(seed: int = 0) -> dict:
    key = jax.random.key(seed)
    ks = jax.random.split(key, 24)

    def w(k, shape, fan_in):
        return jax.random.normal(k, shape, jnp.float32) * (fan_in ** -0.5)

    def gain(k, shape):
        return 1.0 + 0.02 * jax.random.normal(k, shape, jnp.float32)

    x = jax.random.normal(ks[0], (BATCH, SEQ, D_MODEL), jnp.float32)
    p = jax.random.normal(ks[1], (DEPTH, BATCH, SEQ, PLE_DIM), jnp.float32)
    offs = jax.random.randint(ks[2], (BATCH, 1), 0, 1024, dtype=jnp.int32)
    positions = offs + jnp.arange(SEQ, dtype=jnp.int32)[None, :]
    return {
        "x": x,
        "p": p,
        "positions": positions,
        "mla_ln_g": gain(ks[3], (N_A, D_MODEL)),
        "mla_w_in": w(ks[4], (N_A, D_MODEL, MLA_IN), D_MODEL),
        "mla_q_norm_g": gain(ks[5], (N_A, Q_LORA)),
        "mla_kv_norm_g": gain(ks[6], (N_A, KV_LORA)),
        "mla_w_q_up": w(ks[7], (N_A, Q_LORA, N_HEADS * QK_DIM), Q_LORA),
        "mla_w_kv_up": w(ks[8], (N_A, KV_LORA, N_HEADS * 2 * HEAD_DIM), KV_LORA),
        "mla_q_head_g": gain(ks[9], (N_A, QK_DIM)),
        "mla_k_head_g": gain(ks[10], (N_A, QK_DIM)),
        "mla_w_out": w(ks[11], (N_A, MIX_WIDTH, D_MODEL), MIX_WIDTH),
        "kv_ln_g": gain(ks[12], (D_MODEL,)),
        "w_kv_shared": w(ks[13], (D_MODEL, 2 * MIX_WIDTH), D_MODEL),
        "sb_ln_g": gain(ks[14], (N_B, D_MODEL)),
        "sb_w_in": w(ks[15], (N_B, D_MODEL, SB_IN), D_MODEL),
        "sb_w_out": w(ks[16], (N_B, MIX_WIDTH, D_MODEL), MIX_WIDTH),
        "ple_w_proj": w(ks[17], (DEPTH, PLE_DIM, D_MODEL), PLE_DIM),
        "ple_w_gate": w(ks[18], (DEPTH, D_MODEL, D_MODEL), D_MODEL),
    }


def reference(x, p, positions, mla_ln_g, mla_w_in, mla_q_norm_g, mla_kv_norm_g,
              mla_w_q_up, mla_w_kv_up, mla_q_head_g, mla_k_head_g, mla_w_out,
              kv_ln_g, w_kv_shared, sb_ln_g, sb_w_in, sb_w_out,
              ple_w_proj, ple_w_gate):
    B, S, _ = x.shape
    k_sh = v_sh = None
    for i in range(DEPTH):
        if i < N_A:
            x = mla_layer(x, positions, mla_ln_g[i], mla_w_in[i], mla_q_norm_g[i],
                          mla_kv_norm_g[i], mla_w_q_up[i], mla_w_kv_up[i],
                          mla_q_head_g[i], mla_k_head_g[i], mla_w_out[i])
        else:
            j = i - N_A
            x = sb_layer(x, k_sh, v_sh, sb_ln_g[j], sb_w_in[j], sb_w_out[j])
        x = x + jax.nn.sigmoid(x @ ple_w_gate[i]) * (p[i] @ ple_w_proj[i])
        if i == N_A - 1:
            kv = rmsnorm(x, kv_ln_g) @ w_kv_shared
            k_sh = kv[..., :MIX_WIDTH].reshape(B, S, N_HEADS, HEAD_DIM).transpose(0, 2, 1, 3)
            v_sh = kv[..., MIX_WIDTH:].reshape(B, S, N_HEADS, HEAD_DIM).transpose(0, 2, 1, 3)
    return x
```

```python
import functools

import jax
import jax.numpy as jnp
from jax import lax
from jax.experimental import pallas as pl
from jax.experimental.pallas import tpu as pltpu

D_MODEL = 1024
DEPTH = 4
N_A = DEPTH // 2
PLE_DIM = 256
N_HEADS = 16
HEAD_DIM = 64
ROPE_DIM = 32
ROPE_HALF = ROPE_DIM // 2
QK_DIM = HEAD_DIM + ROPE_DIM
Q_LORA = 384
KV_LORA = 256
MIX_WIDTH = N_HEADS * HEAD_DIM
ROPE_THETA = 10000.0
EPS = 1e-6

LANES = 128
HEADS_PER_BLOCK = LANES // HEAD_DIM
N_PAIRS = N_HEADS // HEADS_PER_BLOCK
MLA_PAD = N_HEADS * LANES
TOKEN_TILE = 256
ATTN_TILE = 256
VMEM_LIMIT = 48 * 1024 * 1024

F32 = jnp.float32
BF16 = jnp.bfloat16


def _dot(a, b):
    return jnp.dot(a, b, preferred_element_type=F32)


def _dot_nt(a, b):
    return lax.dot_general(a, b, (((1,), (1,)), ((), ())), preferred_element_type=F32)


def _rms(x, g):
    return x * lax.rsqrt(jnp.mean(x * x, axis=-1, keepdims=True) + EPS) * g


def _sigmoid(x):
    return 1.0 / (1.0 + jnp.exp(-x))


def _params():
    return pltpu.CompilerParams(dimension_semantics=("arbitrary",), vmem_limit_bytes=VMEM_LIMIT)


def _const_spec(shape):
    return pl.BlockSpec(shape, lambda i: (0,) * len(shape))


def _row_spec(width, tile=TOKEN_TILE):
    return pl.BlockSpec((tile, width), lambda i: (i, 0))


def _rope_table_kernel(pos_ref, inv_ref, sign_ref, cos_ref, sin_ref):
    ang = pos_ref[...] * inv_ref[...]
    cos_ref[...] = jnp.cos(ang)
    sin_ref[...] = jnp.sin(ang) * sign_ref[...]


def _rope_tables(positions):
    tokens = positions.size
    pos = positions.astype(F32).reshape(tokens, 1)
    inv = 1.0 / (ROPE_THETA ** (jnp.arange(ROPE_HALF, dtype=F32) / ROPE_HALF))
    zeros = jnp.zeros((HEAD_DIM,), F32)
    pad = jnp.zeros((LANES - QK_DIM,), F32)
    inv_row = jnp.concatenate([zeros, inv, inv, pad]).reshape(1, LANES)
    ones = jnp.ones((ROPE_HALF,), F32)
    sign_row = jnp.concatenate([zeros, -ones, ones, pad]).reshape(1, LANES)
    out = jax.ShapeDtypeStruct((tokens, LANES), F32)
    return pl.pallas_call(
        _rope_table_kernel,
        grid=(tokens // TOKEN_TILE,),
        in_specs=[_row_spec(1), _const_spec((1, LANES)), _const_spec((1, LANES))],
        out_specs=[_row_spec(LANES), _row_spec(LANES)],
        out_shape=[out, out],
        compiler_params=_params(),
        name="rope_tables",
    )(pos, inv_row, sign_row)


def _mla_front_kernel(x_ref, ln_ref, wcq_ref, wckv_ref, wkr_ref, wgate_ref, qn_ref, kvn_ref,
                      wq_ref, wk_ref, wv_ref, gq_ref, gk_ref, cos_ref, sin_ref,
                      q_ref, k_ref, v_ref, gate_ref):
    hb = _rms(x_ref[...], ln_ref[...]).astype(BF16)
    gate_ref[...] = _dot(hb, wgate_ref[...]).astype(BF16)
    cqn = _rms(_dot(hb, wcq_ref[...]), qn_ref[...]).astype(BF16)
    ckvn = _rms(_dot(hb, wckv_ref[...]), kvn_ref[...]).astype(BF16)
    kr = _dot(hb, wkr_ref[...])
    v_ref[...] = _dot(ckvn, wv_ref[...]).astype(BF16)

    cos = cos_ref[...]
    sin = sin_ref[...]
    lane = lax.broadcasted_iota(jnp.int32, cos.shape, 1)
    first_half = lane < HEAD_DIM + ROPE_HALF

    def rope(y):
        partner = jnp.where(first_half, pltpu.roll(y, LANES - ROPE_HALF, 1),
                            pltpu.roll(y, ROPE_HALF, 1))
        return y * cos + partner * sin

    gq = gq_ref[...]
    gk = gk_ref[...]
    k_rope = rope(kr * gk)
    ss_kr = jnp.sum(kr * kr, axis=-1, keepdims=True)
    q_scale = QK_DIM ** -0.5
    for p in range(N_PAIRS):
        cols = slice(p * 2 * LANES, (p + 1) * 2 * LANES)
        kn2 = _dot(ckvn, wk_ref[:, cols])
        qh2 = _dot(cqn, wq_ref[:, cols])
        for j in range(HEADS_PER_BLOCK):
            blk = slice(j * LANES, (j + 1) * LANES)
            out = slice((p * 2 + j) * LANES, (p * 2 + j + 1) * LANES)
            kn = kn2[:, blk]
            ss = jnp.sum(kn * kn, axis=-1, keepdims=True) + ss_kr
            r = lax.rsqrt(ss * (1.0 / QK_DIM) + EPS)
            k_ref[:, out] = ((kn * gk + k_rope) * r).astype(BF16)
            qh = qh2[:, blk]
            ss = jnp.sum(qh * qh, axis=-1, keepdims=True)
            r = lax.rsqrt(ss * (1.0 / QK_DIM) + EPS) * q_scale
            q_ref[:, out] = (rope(qh * gq) * r).astype(BF16)


def _head_block_row(g):
    return jnp.concatenate([g, jnp.zeros((LANES - QK_DIM,), F32)]).reshape(1, LANES)


def _pad_heads(w, width):
    k = w.shape[0]
    w = w.reshape(k, N_HEADS, width)
    w = jnp.pad(w, ((0, 0), (0, 0), (0, LANES - width)))
    return w.reshape(k, MLA_PAD)


def _mla_front(x, ln_g, w_in, qn_g, kvn_g, w_q_up, w_kv_up, q_head_g, k_head_g, cos_t, sin_t):
    tokens = x.shape[0]
    wcq = w_in[:, :Q_LORA].astype(BF16)
    wckv = w_in[:, Q_LORA:Q_LORA + KV_LORA].astype(BF16)
    wkr = w_in[:, Q_LORA + KV_LORA:Q_LORA + KV_LORA + ROPE_DIM]
    wkr = jnp.pad(wkr, ((0, 0), (HEAD_DIM, LANES - QK_DIM))).astype(BF16)
    wgate = w_in[:, Q_LORA + KV_LORA + ROPE_DIM:].astype(BF16)
    wq = _pad_heads(w_q_up, QK_DIM).astype(BF16)
    wkv = w_kv_up.reshape(KV_LORA, N_HEADS, 2 * HEAD_DIM)
    wk = _pad_heads(wkv[:, :, :HEAD_DIM].reshape(KV_LORA, MIX_WIDTH), HEAD_DIM).astype(BF16)
    wv = wkv[:, :, HEAD_DIM:].reshape(KV_LORA, MIX_WIDTH).astype(BF16)
    out = lambda w: jax.ShapeDtypeStruct((tokens, w), BF16)
    return pl.pallas_call(
        _mla_front_kernel,
        grid=(tokens // TOKEN_TILE,),
        in_specs=[_row_spec(D_MODEL), _const_spec((1, D_MODEL)),
                  _const_spec((D_MODEL, Q_LORA)), _const_spec((D_MODEL, KV_LORA)),
                  _const_spec((D_MODEL, LANES)), _const_spec((D_MODEL, MIX_WIDTH)),
                  _const_spec((1, Q_LORA)), _const_spec((1, KV_LORA)),
                  _const_spec((Q_LORA, MLA_PAD)), _const_spec((KV_LORA, MLA_PAD)),
                  _const_spec((KV_LORA, MIX_WIDTH)),
                  _const_spec((1, LANES)), _const_spec((1, LANES)),
                  _row_spec(LANES), _row_spec(LANES)],
        out_specs=[_row_spec(MLA_PAD), _row_spec(MLA_PAD), _row_spec(MIX_WIDTH), _row_spec(MIX_WIDTH)],
        out_shape=[out(MLA_PAD), out(MLA_PAD), out(MIX_WIDTH), out(MIX_WIDTH)],
        compiler_params=_params(),
        name="mla_front",
    )(x, ln_g.reshape(1, D_MODEL), wcq, wckv, wkr, wgate, qn_g.reshape(1, Q_LORA),
      kvn_g.reshape(1, KV_LORA), wq, wk, wv, _head_block_row(q_head_g), _head_block_row(k_head_g),
      cos_t, sin_t)


def _attn_params():
    return pltpu.CompilerParams(dimension_semantics=("arbitrary",) * 3, vmem_limit_bytes=VMEM_LIMIT)


def _first_head_lanes(shape):
    return lax.broadcasted_iota(jnp.int32, shape, 1) < HEAD_DIM


def _mla_attn_kernel(q_ref, k_ref, v_ref, o_ref):
    qi = pl.program_id(2)
    t = ATTN_TILE
    row = lax.broadcasted_iota(jnp.int32, (t, t), 0)
    col = lax.broadcasted_iota(jnp.int32, (t, t), 1)
    causal = col <= row
    outs = []
    for j in range(HEADS_PER_BLOCK):
        blk = slice(j * LANES, (j + 1) * LANES)
        q = q_ref[0, :, blk]

        def step(kj, carry, masked, q=q, blk=blk):
            m, l, acc = carry
            start = pl.multiple_of(kj * t, t)
            s = _dot_nt(q, k_ref[0, pl.ds(start, t), blk])
            if masked:
                s = jnp.where(causal, s, -jnp.inf)
            m_new = jnp.maximum(m, jnp.max(s, axis=-1, keepdims=True))
            p = jnp.exp(s - m_new)
            alpha = jnp.exp(m - m_new)
            l = alpha * l + jnp.sum(p, axis=-1, keepdims=True)
            acc = alpha * acc + _dot(p.astype(BF16), v_ref[0, pl.ds(start, t), :])
            return m_new, l, acc

        carry = (jnp.full((t, 1), -jnp.inf, F32), jnp.zeros((t, 1), F32), jnp.zeros((t, LANES), F32))
        carry = lax.fori_loop(0, qi, functools.partial(step, masked=False), carry)
        _, l, acc = step(qi, carry, True)
        outs.append(acc / l)
    o_ref[0] = jnp.where(_first_head_lanes((t, LANES)), outs[0], outs[1]).astype(BF16)


def _mla_attention(q, k, v, batch, seq):
    q = q.reshape(batch, seq, MLA_PAD)
    k = k.reshape(batch, seq, MLA_PAD)
    v = v.reshape(batch, seq, MIX_WIDTH)
    o = pl.pallas_call(
        _mla_attn_kernel,
        grid=(batch, N_PAIRS, seq // ATTN_TILE),
        in_specs=[pl.BlockSpec((1, ATTN_TILE, 2 * LANES), lambda b, p, i: (b, i, p)),
                  pl.BlockSpec((1, seq, 2 * LANES), lambda b, p, i: (b, 0, p)),
                  pl.BlockSpec((1, seq, LANES), lambda b, p, i: (b, 0, p))],
        out_specs=pl.BlockSpec((1, ATTN_TILE, LANES), lambda b, p, i: (b, i, p)),
        out_shape=jax.ShapeDtypeStruct((batch, seq, MIX_WIDTH), BF16),
        compiler_params=_attn_params(),
        name="mla_attention",
    )(q, k, v)
    return o.reshape(batch * seq, MIX_WIDTH)


def _sb_attn_kernel(q_ref, k_ref, v_ref, o_ref):
    qi = pl.program_id(2)
    t = ATTN_TILE
    row = lax.broadcasted_iota(jnp.int32, (t, t), 0)
    col = lax.broadcasted_iota(jnp.int32, (t, t), 1)
    strict = col < row
    suffix = jnp.where(row > col, 1.0, 0.0).astype(BF16)
    first = _first_head_lanes((t, LANES))
    q2 = q_ref[0]
    outs = []
    for j in range(HEADS_PER_BLOCK):
        head = first if j == 0 else jnp.logical_not(first)
        q = jnp.where(head, q2, jnp.zeros_like(q2))

        def step(kj, carry, masked, q=q):
            later, acc = carry
            start = pl.multiple_of(kj * t, t)
            z = _dot_nt(q, k_ref[0, pl.ds(start, t), :])
            soft = jnp.log(1.0 + jnp.exp(-jnp.abs(z)))
            log_beta = jnp.minimum(z, 0.0) - soft
            log_rest = jnp.minimum(-z, 0.0) - soft
            if masked:
                log_rest = jnp.where(strict, log_rest, 0.0)
            hi = log_rest.astype(BF16)
            lo = (log_rest - hi.astype(F32)).astype(BF16)
            tail = _dot(hi, suffix) + _dot(lo, suffix) + later
            a = jnp.exp(log_beta + tail)
            if masked:
                a = jnp.where(strict, a, 0.0)
            acc = acc + _dot(a.astype(BF16), v_ref[0, pl.ds(start, t), :])
            later = later + jnp.sum(log_rest, axis=-1, keepdims=True)
            return later, acc

        carry = step(qi, (jnp.zeros((t, 1), F32), jnp.zeros((t, LANES), F32)), True)
        _, acc = lax.fori_loop(0, qi, lambda i, c: step(qi - 1 - i, c, False), carry)
        outs.append(acc)
    o_ref[0] = jnp.where(first, outs[0], outs[1]).astype(BF16)


def _sb_attention(q, k, v, batch, seq):
    q = q.reshape(batch, seq, MIX_WIDTH)
    k = k.reshape(batch, seq, MIX_WIDTH)
    v = v.reshape(batch, seq, MIX_WIDTH)
    whole = pl.BlockSpec((1, seq, LANES), lambda b, p, i: (b, 0, p))
    tile = pl.BlockSpec((1, ATTN_TILE, LANES), lambda b, p, i: (b, i, p))
    o = pl.pallas_call(
        _sb_attn_kernel,
        grid=(batch, N_PAIRS, seq // ATTN_TILE),
        in_specs=[tile, whole, whole],
        out_specs=tile,
        out_shape=jax.ShapeDtypeStruct((batch, seq, MIX_WIDTH), BF16),
        compiler_params=_attn_params(),
        name="sb_attention",
    )(q, k, v)
    return o.reshape(batch * seq, MIX_WIDTH)


def _post_kernel(o_ref, gate_ref, x_ref, p_ref, wout_ref, wg_ref, wp_ref, *rest, with_kv):
    gate = gate_ref[...].astype(F32)
    u = o_ref[...].astype(F32) * (gate * _sigmoid(gate))
    y = x_ref[...] + _dot(u.astype(BF16), wout_ref[...])
    ple = _dot(p_ref[...].astype(BF16), wp_ref[...])
    x_new = y + _sigmoid(_dot(y.astype(BF16), wg_ref[...])) * ple
    if with_kv:
        kvg_ref, wkv_ref, xo_ref, k_ref, v_ref = rest
        kv = _dot(_rms(x_new, kvg_ref[...]).astype(BF16), wkv_ref[...])
        k_ref[...] = kv[:, :MIX_WIDTH].astype(BF16)
        v_ref[...] = kv[:, MIX_WIDTH:].astype(BF16)
    else:
        (xo_ref,) = rest
    xo_ref[...] = x_new


def _post(o, gate, x, p, w_out, w_gate, w_proj, kv_ln_g=None, w_kv=None):
    tokens = x.shape[0]
    with_kv = w_kv is not None
    in_specs = [_row_spec(MIX_WIDTH), _row_spec(MIX_WIDTH), _row_spec(D_MODEL), _row_spec(PLE_DIM),
                _const_spec((MIX_WIDTH, D_MODEL)), _const_spec((D_MODEL, D_MODEL)),
                _const_spec((PLE_DIM, D_MODEL))]
    args = [o, gate, x, p, w_out.astype(BF16), w_gate.astype(BF16), w_proj.astype(BF16)]
    out_specs = [_row_spec(D_MODEL)]
    out_shape = [jax.ShapeDtypeStruct((tokens, D_MODEL), F32)]
    if with_kv:
        in_specs += [_const_spec((1, D_MODEL)), _const_spec((D_MODEL, 2 * MIX_WIDTH))]
        args += [kv_ln_g.reshape(1, D_MODEL), w_kv.astype(BF16)]
        out_specs += [_row_spec(MIX_WIDTH), _row_spec(MIX_WIDTH)]
        out_shape += [jax.ShapeDtypeStruct((tokens, MIX_WIDTH), BF16)] * 2
    return pl.pallas_call(
        functools.partial(_post_kernel, with_kv=with_kv),
        grid=(tokens // TOKEN_TILE,),
        in_specs=in_specs, out_specs=out_specs, out_shape=out_shape,
        compiler_params=_params(),
        name="layer_tail_kv" if with_kv else "layer_tail",
    )(*args)


def _sb_front_kernel(x_ref, ln_ref, wq_ref, wgate_ref, q_ref, gate_ref):
    hb = _rms(x_ref[...], ln_ref[...]).astype(BF16)
    q_ref[...] = (_dot(hb, wq_ref[...]) * (HEAD_DIM ** -0.5)).astype(BF16)
    gate_ref[...] = _dot(hb, wgate_ref[...]).astype(BF16)


def _sb_front(x, ln_g, w_in):
    tokens = x.shape[0]
    out = jax.ShapeDtypeStruct((tokens, MIX_WIDTH), BF16)
    return pl.pallas_call(
        _sb_front_kernel,
        grid=(tokens // TOKEN_TILE,),
        in_specs=[_row_spec(D_MODEL), _const_spec((1, D_MODEL)),
                  _const_spec((D_MODEL, MIX_WIDTH)), _const_spec((D_MODEL, MIX_WIDTH))],
        out_specs=[_row_spec(MIX_WIDTH), _row_spec(MIX_WIDTH)],
        out_shape=[out, out],
        compiler_params=_params(),
        name="sb_front",
    )(x, ln_g.reshape(1, D_MODEL), w_in[:, :MIX_WIDTH].astype(BF16), w_in[:, MIX_WIDTH:].astype(BF16))


def kernel(x, p, positions, mla_ln_g, mla_w_in, mla_q_norm_g, mla_kv_norm_g, mla_w_q_up, mla_w_kv_up, mla_q_head_g, mla_k_head_g, mla_w_out, kv_ln_g, w_kv_shared, sb_ln_g, sb_w_in, sb_w_out, ple_w_proj, ple_w_gate):
    batch, seq, _ = x.shape
    tokens = batch * seq
    x = x.reshape(tokens, D_MODEL)
    p = p.reshape(DEPTH, tokens, PLE_DIM)
    cos_t, sin_t = _rope_tables(positions)
    k_sh = v_sh = None
    for i in range(DEPTH):
        if i < N_A:
            q, k, v, gate = _mla_front(x, mla_ln_g[i], mla_w_in[i], mla_q_norm_g[i], mla_kv_norm_g[i],
                                       mla_w_q_up[i], mla_w_kv_up[i], mla_q_head_g[i], mla_k_head_g[i],
                                       cos_t, sin_t)
            o = _mla_attention(q, k, v, batch, seq)
            w_out = mla_w_out[i]
        else:
            j = i - N_A
            q, gate = _sb_front(x, sb_ln_g[j], sb_w_in[j])
            o = _sb_attention(q, k_sh, v_sh, batch, seq)
            w_out = sb_w_out[j]
        if i == N_A - 1:
            x, k_sh, v_sh = _post(o, gate, x, p[i], w_out, ple_w_gate[i], ple_w_proj[i], kv_ln_g, w_kv_shared)
        else:
            (x,) = _post(o, gate, x, p[i], w_out, ple_w_gate[i], ple_w_proj[i])
    return x.reshape(batch, seq, D_MODEL)
```

```python
import functools

import jax
import jax.numpy as jnp
from jax import lax
from jax.experimental import pallas as pl
from jax.experimental.pallas import tpu as pltpu

D_MODEL = 1024
DEPTH = 4
N_A = DEPTH // 2
PLE_DIM = 256
N_HEADS = 16
HEAD_DIM = 64
ROPE_DIM = 32
ROPE_HALF = ROPE_DIM // 2
QK_DIM = HEAD_DIM + ROPE_DIM
Q_LORA = 384
KV_LORA = 256
MIX_WIDTH = N_HEADS * HEAD_DIM
ROPE_THETA = 10000.0
EPS = 1e-6

LANES = 128
HEADS_PER_BLOCK = LANES // HEAD_DIM
N_PAIRS = N_HEADS // HEADS_PER_BLOCK
MLA_PAD = N_HEADS * LANES
TOKEN_TILE = 256
ATTN_TQ = 256
KEY_BLOCK = 512
SB_SUB_BLOCK = 256
MLA_HEADS_PER_STEP = 4
LOG2_E = 1.4426950408889634
VMEM_LIMIT = 48 * 1024 * 1024

F32 = jnp.float32
BF16 = jnp.bfloat16


def _dot(a, b):
    return jnp.dot(a, b, preferred_element_type=F32)


def _dot_nt(a, b):
    return lax.dot_general(a, b, (((1,), (1,)), ((), ())), preferred_element_type=F32)


def _rms(x, g):
    return x * lax.rsqrt(jnp.mean(x * x, axis=-1, keepdims=True) + EPS) * g


def _sigmoid(x):
    return 1.0 / (1.0 + jnp.exp(-x))


def _params():
    return pltpu.CompilerParams(dimension_semantics=("arbitrary",), vmem_limit_bytes=VMEM_LIMIT)


def _const_spec(shape):
    return pl.BlockSpec(shape, lambda i: (0,) * len(shape))


def _row_spec(width, tile=TOKEN_TILE):
    return pl.BlockSpec((tile, width), lambda i: (i, 0))


def _rope_table_kernel(pos_ref, inv_ref, sign_ref, cos_ref, sin_ref):
    ang = pos_ref[...] * inv_ref[...]
    cos_ref[...] = jnp.cos(ang)
    sin_ref[...] = jnp.sin(ang) * sign_ref[...]


def _rope_tables(positions):
    tokens = positions.size
    pos = positions.astype(F32).reshape(tokens, 1)
    inv = 1.0 / (ROPE_THETA ** (jnp.arange(ROPE_HALF, dtype=F32) / ROPE_HALF))
    zeros = jnp.zeros((HEAD_DIM,), F32)
    pad = jnp.zeros((LANES - QK_DIM,), F32)
    inv_row = jnp.concatenate([zeros, inv, inv, pad]).reshape(1, LANES)
    ones = jnp.ones((ROPE_HALF,), F32)
    sign_row = jnp.concatenate([zeros, -ones, ones, pad]).reshape(1, LANES)
    out = jax.ShapeDtypeStruct((tokens, LANES), F32)
    return pl.pallas_call(
        _rope_table_kernel,
        grid=(tokens // TOKEN_TILE,),
        in_specs=[_row_spec(1), _const_spec((1, LANES)), _const_spec((1, LANES))],
        out_specs=[_row_spec(LANES), _row_spec(LANES)],
        out_shape=[out, out],
        compiler_params=_params(),
        name="rope_tables",
    )(pos, inv_row, sign_row)


def _mla_front_kernel(x_ref, ln_ref, wcq_ref, wckv_ref, wkr_ref, wgate_ref, qn_ref, kvn_ref,
                      wq_ref, wk_ref, wv_ref, gq_ref, gk_ref, cos_ref, sin_ref,
                      q_ref, k_ref, v_ref, gate_ref):
    hb = _rms(x_ref[...], ln_ref[...]).astype(BF16)
    gate_ref[...] = _dot(hb, wgate_ref[...]).astype(BF16)
    cqn = _rms(_dot(hb, wcq_ref[...]), qn_ref[...]).astype(BF16)
    ckvn = _rms(_dot(hb, wckv_ref[...]), kvn_ref[...]).astype(BF16)
    kr = _dot(hb, wkr_ref[...])
    v_ref[...] = _dot(ckvn, wv_ref[...]).astype(BF16)

    cos = cos_ref[...]
    sin = sin_ref[...]
    lane = lax.broadcasted_iota(jnp.int32, cos.shape, 1)
    first_half = lane < HEAD_DIM + ROPE_HALF

    def rope(y):
        partner = jnp.where(first_half, pltpu.roll(y, LANES - ROPE_HALF, 1),
                            pltpu.roll(y, ROPE_HALF, 1))
        return y * cos + partner * sin

    gq = gq_ref[...]
    gk = gk_ref[...]
    k_rope = rope(kr * gk)
    ss_kr = jnp.sum(kr * kr, axis=-1, keepdims=True)
    q_scale = QK_DIM ** -0.5 * LOG2_E
    for p in range(N_PAIRS):
        cols = slice(p * 2 * LANES, (p + 1) * 2 * LANES)
        kn2 = _dot(ckvn, wk_ref[:, cols])
        qh2 = _dot(cqn, wq_ref[:, cols])
        for j in range(HEADS_PER_BLOCK):
            blk = slice(j * LANES, (j + 1) * LANES)
            out = slice((p * 2 + j) * LANES, (p * 2 + j + 1) * LANES)
            kn = kn2[:, blk]
            ss = jnp.sum(kn * kn, axis=-1, keepdims=True) + ss_kr
            r = lax.rsqrt(ss * (1.0 / QK_DIM) + EPS)
            k_ref[:, out] = ((kn * gk + k_rope) * r).astype(BF16)
            qh = qh2[:, blk]
            ss = jnp.sum(qh * qh, axis=-1, keepdims=True)
            r = lax.rsqrt(ss * (1.0 / QK_DIM) + EPS) * q_scale
            q_ref[:, out] = (rope(qh * gq) * r).astype(BF16)


def _head_block_row(g):
    return jnp.concatenate([g, jnp.zeros((LANES - QK_DIM,), F32)]).reshape(1, LANES)


def _pad_heads(w, width):
    k = w.shape[0]
    w = w.reshape(k, N_HEADS, width)
    w = jnp.pad(w, ((0, 0), (0, 0), (0, LANES - width)))
    return w.reshape(k, MLA_PAD)


def _mla_front(x, ln_g, w_in, qn_g, kvn_g, w_q_up, w_kv_up, q_head_g, k_head_g, cos_t, sin_t):
    tokens = x.shape[0]
    wcq = w_in[:, :Q_LORA].astype(BF16)
    wckv = w_in[:, Q_LORA:Q_LORA + KV_LORA].astype(BF16)
    wkr = w_in[:, Q_LORA + KV_LORA:Q_LORA + KV_LORA + ROPE_DIM]
    wkr = jnp.pad(wkr, ((0, 0), (HEAD_DIM, LANES - QK_DIM))).astype(BF16)
    wgate = w_in[:, Q_LORA + KV_LORA + ROPE_DIM:].astype(BF16)
    wq = _pad_heads(w_q_up, QK_DIM).astype(BF16)
    wkv = w_kv_up.reshape(KV_LORA, N_HEADS, 2 * HEAD_DIM)
    wk = _pad_heads(wkv[:, :, :HEAD_DIM].reshape(KV_LORA, MIX_WIDTH), HEAD_DIM).astype(BF16)
    wv = wkv[:, :, HEAD_DIM:].reshape(KV_LORA, MIX_WIDTH).astype(BF16)
    out = lambda w: jax.ShapeDtypeStruct((tokens, w), BF16)
    return pl.pallas_call(
        _mla_front_kernel,
        grid=(tokens // TOKEN_TILE,),
        in_specs=[_row_spec(D_MODEL), _const_spec((1, D_MODEL)),
                  _const_spec((D_MODEL, Q_LORA)), _const_spec((D_MODEL, KV_LORA)),
                  _const_spec((D_MODEL, LANES)), _const_spec((D_MODEL, MIX_WIDTH)),
                  _const_spec((1, Q_LORA)), _const_spec((1, KV_LORA)),
                  _const_spec((Q_LORA, MLA_PAD)), _const_spec((KV_LORA, MLA_PAD)),
                  _const_spec((KV_LORA, MIX_WIDTH)),
                  _const_spec((1, LANES)), _const_spec((1, LANES)),
                  _row_spec(LANES), _row_spec(LANES)],
        out_specs=[_row_spec(MLA_PAD), _row_spec(MLA_PAD), _row_spec(MIX_WIDTH), _row_spec(MIX_WIDTH)],
        out_shape=[out(MLA_PAD), out(MLA_PAD), out(MIX_WIDTH), out(MIX_WIDTH)],
        compiler_params=_params(),
        name="mla_front",
    )(x, ln_g.reshape(1, D_MODEL), wcq, wckv, wkr, wgate, qn_g.reshape(1, Q_LORA),
      kvn_g.reshape(1, KV_LORA), wq, wk, wv, _head_block_row(q_head_g), _head_block_row(k_head_g),
      cos_t, sin_t)


def _attn_params():
    return pltpu.CompilerParams(dimension_semantics=("arbitrary",) * 3, vmem_limit_bytes=VMEM_LIMIT)


def _first_head_lanes(shape):
    return lax.broadcasted_iota(jnp.int32, shape, 1) < HEAD_DIM


def _key_block_plan(qi):
    ratio = KEY_BLOCK // ATTN_TQ
    n_full = qi // ratio
    diag_off = (qi - n_full * ratio) * ATTN_TQ
    return n_full, diag_off


def _mla_attn_kernel(q_ref, k_ref, v_ref, o_ref):
    n_full, diag_off = _key_block_plan(pl.program_id(2))
    row = lax.broadcasted_iota(jnp.int32, (ATTN_TQ, KEY_BLOCK), 0)
    col = lax.broadcasted_iota(jnp.int32, (ATTN_TQ, KEY_BLOCK), 1)
    causal = col <= row + diag_off
    heads = range(MLA_HEADS_PER_STEP)
    qs = [q_ref[0, :, h * LANES:(h + 1) * LANES] for h in heads]

    def step(start, carry, masked):
        scores = [_dot_nt(qs[h], k_ref[0, pl.ds(start, KEY_BLOCK), h * LANES:(h + 1) * LANES])
                  for h in heads]
        probs = []
        for h in heads:
            m, l, _ = carry[h]
            s = jnp.where(causal, scores[h], -jnp.inf) if masked else scores[h]
            m_new = jnp.maximum(m, jnp.max(s, axis=-1, keepdims=True))
            p = jnp.exp2(s - m_new)
            alpha = jnp.exp2(m - m_new)
            l = alpha * l + jnp.sum(p, axis=-1, keepdims=True)
            probs.append((m_new, l, alpha, p.astype(BF16)))
        new = []
        for h in heads:
            m_new, l, alpha, p = probs[h]
            pair = h // HEADS_PER_BLOCK
            v = v_ref[0, pl.ds(start, KEY_BLOCK), pair * LANES:(pair + 1) * LANES]
            new.append((m_new, l, alpha * carry[h][2] + _dot(p, v)))
        return tuple(new)

    init = tuple((jnp.full((ATTN_TQ, 1), -jnp.inf, F32), jnp.zeros((ATTN_TQ, 1), F32),
                  jnp.zeros((ATTN_TQ, LANES), F32)) for _ in heads)
    carry = lax.fori_loop(
        0, n_full, lambda kb, c: step(pl.multiple_of(kb * KEY_BLOCK, KEY_BLOCK), c, False), init)
    carry = step(pl.multiple_of(n_full * KEY_BLOCK, KEY_BLOCK), carry, True)
    first = _first_head_lanes((ATTN_TQ, LANES))
    for b in range(MLA_HEADS_PER_STEP // HEADS_PER_BLOCK):
        (_, l0, acc0), (_, l1, acc1) = carry[2 * b], carry[2 * b + 1]
        o_ref[0, :, b * LANES:(b + 1) * LANES] = jnp.where(first, acc0 / l0, acc1 / l1).astype(BF16)


def _mla_attention(q, k, v, batch, seq):
    q = q.reshape(batch, seq, MLA_PAD)
    k = k.reshape(batch, seq, MLA_PAD)
    v = v.reshape(batch, seq, MIX_WIDTH)
    qk_w = MLA_HEADS_PER_STEP * LANES
    v_w = MLA_HEADS_PER_STEP * HEAD_DIM
    o = pl.pallas_call(
        _mla_attn_kernel,
        grid=(batch, N_HEADS // MLA_HEADS_PER_STEP, seq // ATTN_TQ),
        in_specs=[pl.BlockSpec((1, ATTN_TQ, qk_w), lambda b, g, i: (b, i, g)),
                  pl.BlockSpec((1, seq, qk_w), lambda b, g, i: (b, 0, g)),
                  pl.BlockSpec((1, seq, v_w), lambda b, g, i: (b, 0, g))],
        out_specs=pl.BlockSpec((1, ATTN_TQ, v_w), lambda b, g, i: (b, i, g)),
        out_shape=jax.ShapeDtypeStruct((batch, seq, MIX_WIDTH), BF16),
        compiler_params=_attn_params(),
        name="mla_attention",
    )(q, k, v)
    return o.reshape(batch * seq, MIX_WIDTH)


def _sb_attn_kernel(q_ref, k_ref, v_ref, o_ref):
    n_full, diag_off = _key_block_plan(pl.program_id(2))
    t = SB_SUB_BLOCK
    row = lax.broadcasted_iota(jnp.int32, (ATTN_TQ, t), 0)
    col = lax.broadcasted_iota(jnp.int32, (ATTN_TQ, t), 1)
    srow = lax.broadcasted_iota(jnp.int32, (t, t), 0)
    scol = lax.broadcasted_iota(jnp.int32, (t, t), 1)
    suffix = jnp.where(srow > scol, 1.0, 0.0).astype(BF16)
    first = _first_head_lanes((ATTN_TQ, LANES))
    q2 = q_ref[0]
    qs = [jnp.where(first, q2, jnp.zeros_like(q2)), jnp.where(first, jnp.zeros_like(q2), q2)]

    chains = [(h, part) for h in range(HEADS_PER_BLOCK) for part in reversed(range(KEY_BLOCK // t))]

    def step(start, carry, masked):
        keys = lambda part: pl.ds(start + part * t, t)
        zs = [_dot_nt(qs[h], k_ref[0, keys(part), :]) for h, part in chains]
        log_betas, log_rests, splits = [], [], []
        for (h, part), z in zip(chains, zs):
            soft = jnp.log(1.0 + jnp.exp2(-jnp.abs(z))) * LOG2_E
            log_beta = jnp.minimum(z, 0.0) - soft
            log_rest = log_beta - z
            if masked:
                log_rest = jnp.where(col + part * t < row + diag_off, log_rest, 0.0)
            hi = log_rest.astype(BF16)
            lo = (log_rest - hi.astype(F32)).astype(BF16)
            log_betas.append(log_beta)
            log_rests.append(log_rest)
            splits.append(jnp.concatenate([hi, lo], axis=0))
        sums = [_dot(x, suffix) for x in splits]
        laters = [carry[h][0] for h in range(HEADS_PER_BLOCK)]
        weights = []
        for i, (h, part) in enumerate(chains):
            tail = sums[i][:ATTN_TQ] + sums[i][ATTN_TQ:] + laters[h]
            a = jnp.exp2(log_betas[i] + tail)
            if masked:
                a = jnp.where(col + part * t < row + diag_off, a, 0.0)
            weights.append(a.astype(BF16))
            laters[h] = laters[h] + jnp.sum(log_rests[i], axis=-1, keepdims=True)
        accs = [carry[h][1] for h in range(HEADS_PER_BLOCK)]
        for i, (h, part) in enumerate(chains):
            accs[h] = accs[h] + _dot(weights[i], v_ref[0, keys(part), :])
        return tuple((laters[h], accs[h]) for h in range(HEADS_PER_BLOCK))

    init = tuple((jnp.zeros((ATTN_TQ, 1), F32), jnp.zeros((ATTN_TQ, LANES), F32))
                 for _ in range(HEADS_PER_BLOCK))
    carry = step(pl.multiple_of(n_full * KEY_BLOCK, KEY_BLOCK), init, True)
    carry = lax.fori_loop(
        0, n_full, lambda i, c: step(pl.multiple_of((n_full - 1 - i) * KEY_BLOCK, KEY_BLOCK), c, False), carry)
    o_ref[0] = jnp.where(first, carry[0][1], carry[1][1]).astype(BF16)


def _sb_attention(q, k, v, batch, seq):
    q = q.reshape(batch, seq, MIX_WIDTH)
    k = k.reshape(batch, seq, MIX_WIDTH)
    v = v.reshape(batch, seq, MIX_WIDTH)
    whole = pl.BlockSpec((1, seq, LANES), lambda b, p, i: (b, 0, p))
    tile = pl.BlockSpec((1, ATTN_TQ, LANES), lambda b, p, i: (b, i, p))
    o = pl.pallas_call(
        _sb_attn_kernel,
        grid=(batch, N_PAIRS, seq // ATTN_TQ),
        in_specs=[tile, whole, whole],
        out_specs=tile,
        out_shape=jax.ShapeDtypeStruct((batch, seq, MIX_WIDTH), BF16),
        compiler_params=_attn_params(),
        name="sb_attention",
    )(q, k, v)
    return o.reshape(batch * seq, MIX_WIDTH)


def _post_kernel(o_ref, gate_ref, x_ref, p_ref, wout_ref, wg_ref, wp_ref, *rest, with_kv):
    gate = gate_ref[...].astype(F32)
    u = o_ref[...].astype(F32) * (gate * _sigmoid(gate))
    y = x_ref[...] + _dot(u.astype(BF16), wout_ref[...])
    ple = _dot(p_ref[...].astype(BF16), wp_ref[...])
    x_new = y + _sigmoid(_dot(y.astype(BF16), wg_ref[...])) * ple
    if with_kv:
        kvg_ref, wkv_ref, xo_ref, k_ref, v_ref = rest
        kv = _dot(_rms(x_new, kvg_ref[...]).astype(BF16), wkv_ref[...])
        k_ref[...] = kv[:, :MIX_WIDTH].astype(BF16)
        v_ref[...] = kv[:, MIX_WIDTH:].astype(BF16)
    else:
        (xo_ref,) = rest
    xo_ref[...] = x_new


def _post(o, gate, x, p, w_out, w_gate, w_proj, kv_ln_g=None, w_kv=None):
    tokens = x.shape[0]
    with_kv = w_kv is not None
    in_specs = [_row_spec(MIX_WIDTH), _row_spec(MIX_WIDTH), _row_spec(D_MODEL), _row_spec(PLE_DIM),
                _const_spec((MIX_WIDTH, D_MODEL)), _const_spec((D_MODEL, D_MODEL)),
                _const_spec((PLE_DIM, D_MODEL))]
    args = [o, gate, x, p, w_out.astype(BF16), w_gate.astype(BF16), w_proj.astype(BF16)]
    out_specs = [_row_spec(D_MODEL)]
    out_shape = [jax.ShapeDtypeStruct((tokens, D_MODEL), F32)]
    if with_kv:
        in_specs += [_const_spec((1, D_MODEL)), _const_spec((D_MODEL, 2 * MIX_WIDTH))]
        args += [kv_ln_g.reshape(1, D_MODEL), w_kv.astype(BF16)]
        out_specs += [_row_spec(MIX_WIDTH), _row_spec(MIX_WIDTH)]
        out_shape += [jax.ShapeDtypeStruct((tokens, MIX_WIDTH), BF16)] * 2
    return pl.pallas_call(
        functools.partial(_post_kernel, with_kv=with_kv),
        grid=(tokens // TOKEN_TILE,),
        in_specs=in_specs, out_specs=out_specs, out_shape=out_shape,
        compiler_params=_params(),
        name="layer_tail_kv" if with_kv else "layer_tail",
    )(*args)


def _sb_front_kernel(x_ref, ln_ref, wq_ref, wgate_ref, q_ref, gate_ref):
    hb = _rms(x_ref[...], ln_ref[...]).astype(BF16)
    q_ref[...] = (_dot(hb, wq_ref[...]) * (HEAD_DIM ** -0.5 * LOG2_E)).astype(BF16)
    gate_ref[...] = _dot(hb, wgate_ref[...]).astype(BF16)


def _sb_front(x, ln_g, w_in):
    tokens = x.shape[0]
    out = jax.ShapeDtypeStruct((tokens, MIX_WIDTH), BF16)
    return pl.pallas_call(
        _sb_front_kernel,
        grid=(tokens // TOKEN_TILE,),
        in_specs=[_row_spec(D_MODEL), _const_spec((1, D_MODEL)),
                  _const_spec((D_MODEL, MIX_WIDTH)), _const_spec((D_MODEL, MIX_WIDTH))],
        out_specs=[_row_spec(MIX_WIDTH), _row_spec(MIX_WIDTH)],
        out_shape=[out, out],
        compiler_params=_params(),
        name="sb_front",
    )(x, ln_g.reshape(1, D_MODEL), w_in[:, :MIX_WIDTH].astype(BF16), w_in[:, MIX_WIDTH:].astype(BF16))


def kernel(x, p, positions, mla_ln_g, mla_w_in, mla_q_norm_g, mla_kv_norm_g, mla_w_q_up, mla_w_kv_up, mla_q_head_g, mla_k_head_g, mla_w_out, kv_ln_g, w_kv_shared, sb_ln_g, sb_w_in, sb_w_out, ple_w_proj, ple_w_gate):
    batch, seq, _ = x.shape
    tokens = batch * seq
    x = x.reshape(tokens, D_MODEL)
    p = p.reshape(DEPTH, tokens, PLE_DIM)
    cos_t, sin_t = _rope_tables(positions)
    k_sh = v_sh = None
    for i in range(DEPTH):
        if i < N_A:
            q, k, v, gate = _mla_front(x, mla_ln_g[i], mla_w_in[i], mla_q_norm_g[i], mla_kv_norm_g[i],
                                       mla_w_q_up[i], mla_w_kv_up[i], mla_q_head_g[i], mla_k_head_g[i],
                                       cos_t, sin_t)
            o = _mla_attention(q, k, v, batch, seq)
            w_out = mla_w_out[i]
        else:
            j = i - N_A
            q, gate = _sb_front(x, sb_ln_g[j], sb_w_in[j])
            o = _sb_attention(q, k_sh, v_sh, batch, seq)
            w_out = sb_w_out[j]
        if i == N_A - 1:
            x, k_sh, v_sh = _post(o, gate, x, p[i], w_out, ple_w_gate[i], ple_w_proj[i], kv_ln_g, w_kv_shared)
        else:
            (x,) = _post(o, gate, x, p[i], w_out, ple_w_gate[i], ple_w_proj[i])
    return x.reshape(batch, seq, D_MODEL)
```

```python
import functools

import jax
import jax.numpy as jnp
from jax import lax
from jax.experimental import pallas as pl
from jax.experimental.pallas import tpu as pltpu

D_MODEL = 1024
DEPTH = 4
N_A = DEPTH // 2
PLE_DIM = 256
N_HEADS = 16
HEAD_DIM = 64
ROPE_DIM = 32
ROPE_HALF = ROPE_DIM // 2
QK_DIM = HEAD_DIM + ROPE_DIM
Q_LORA = 384
KV_LORA = 256
MIX_WIDTH = N_HEADS * HEAD_DIM
ROPE_THETA = 10000.0
EPS = 1e-6

LANES = 128
HEADS_PER_BLOCK = LANES // HEAD_DIM
N_PAIRS = N_HEADS // HEADS_PER_BLOCK
MLA_PAD = N_HEADS * LANES
TOKEN_TILE = 256
ATTN_TQ = 256
KEY_BLOCK = 512
MLA_HEADS_PER_STEP = 4
LOG2_E = 1.4426950408889634
VMEM_LIMIT = 48 * 1024 * 1024

F32 = jnp.float32
BF16 = jnp.bfloat16


def _dot(a, b):
    return jnp.dot(a, b, preferred_element_type=F32)


def _dot_nt(a, b):
    return lax.dot_general(a, b, (((1,), (1,)), ((), ())), preferred_element_type=F32)


def _rms(x, g):
    return x * lax.rsqrt(jnp.mean(x * x, axis=-1, keepdims=True) + EPS) * g


def _sigmoid(x):
    return 1.0 / (1.0 + jnp.exp(-x))


def _params():
    return pltpu.CompilerParams(dimension_semantics=("arbitrary",), vmem_limit_bytes=VMEM_LIMIT)


def _const_spec(shape):
    return pl.BlockSpec(shape, lambda i: (0,) * len(shape))


def _row_spec(width, tile=TOKEN_TILE):
    return pl.BlockSpec((tile, width), lambda i: (i, 0))


def _rope_table_kernel(pos_ref, inv_ref, sign_ref, keep_ref, cos_ref, sin_ref):
    ang = pos_ref[...] * inv_ref[...]
    cos_ref[...] = jnp.cos(ang) * keep_ref[...]
    sin_ref[...] = jnp.sin(ang) * sign_ref[...]


def _head_block_row(nope, first, second):
    pad = jnp.zeros((LANES - QK_DIM - ROPE_HALF,), F32)
    return jnp.concatenate([nope, first, second, first, pad]).reshape(1, LANES)


def _rope_tables(positions):
    tokens = positions.size
    pos = positions.astype(F32).reshape(tokens, 1)
    inv = 1.0 / (ROPE_THETA ** (jnp.arange(ROPE_HALF, dtype=F32) / ROPE_HALF))
    zeros = jnp.zeros((HEAD_DIM,), F32)
    ones = jnp.ones((ROPE_HALF,), F32)
    inv_row = _head_block_row(zeros, inv, inv)
    sign_row = _head_block_row(zeros, -ones, ones) * _keep_row()
    out = jax.ShapeDtypeStruct((tokens, LANES), F32)
    return pl.pallas_call(
        _rope_table_kernel,
        grid=(tokens // TOKEN_TILE,),
        in_specs=[_row_spec(1), _const_spec((1, LANES)), _const_spec((1, LANES)), _const_spec((1, LANES))],
        out_specs=[_row_spec(LANES), _row_spec(LANES)],
        out_shape=[out, out],
        compiler_params=_params(),
        name="rope_tables",
    )(pos, inv_row, sign_row, _keep_row())


def _keep_row():
    return (jnp.arange(LANES) < QK_DIM).astype(F32).reshape(1, LANES)


def _mla_front_kernel(x_ref, ln_ref, wcq_ref, wckv_ref, wkr_ref, wgate_ref, qn_ref, kvn_ref,
                      wq_ref, wk_ref, wv_ref, gq_ref, gk_ref, keep_ref, cos_ref, sin_ref,
                      q_ref, k_ref, v_ref, gate_ref):
    hb = _rms(x_ref[...], ln_ref[...]).astype(BF16)
    cq = _dot(hb, wcq_ref[...])
    ckv = _dot(hb, wckv_ref[...])
    kr = _dot(hb, wkr_ref[...])
    gate_ref[...] = _dot(hb, wgate_ref[...]).astype(BF16)
    cqn = _rms(cq, qn_ref[...]).astype(BF16)
    ckvn = _rms(ckv, kvn_ref[...]).astype(BF16)
    v_ref[...] = _dot(ckvn, wv_ref[...]).astype(BF16)
    pair_cols = [slice(p * 2 * LANES, (p + 1) * 2 * LANES) for p in range(N_PAIRS)]
    kn2 = [_dot(ckvn, wk_ref[:, c]) for c in pair_cols]
    qh2 = [_dot(cqn, wq_ref[:, c]) for c in pair_cols]
    head_blocks = [(p, slice(j * LANES, (j + 1) * LANES))
                   for p in range(N_PAIRS) for j in range(HEADS_PER_BLOCK)]
    kns = [kn2[p][:, blk] for p, blk in head_blocks]
    qhs = [qh2[p][:, blk] for p, blk in head_blocks]

    keep = keep_ref[...]
    cos = cos_ref[...]
    sin = sin_ref[...]
    gq = gq_ref[...]
    gk = gk_ref[...]

    def rope(y):
        return y * cos + pltpu.roll(y, LANES - ROPE_HALF, 1) * sin

    ss_kr = jnp.sum(kr * kr * keep, axis=-1, keepdims=True)
    ss_k = [jnp.sum(kn * kn, axis=-1, keepdims=True) for kn in kns]
    ss_q = [jnp.sum(qh * qh * keep, axis=-1, keepdims=True) for qh in qhs]
    q_scale = QK_DIM ** -0.5 * LOG2_E
    r_k = [lax.rsqrt((ss + ss_kr) * (1.0 / QK_DIM) + EPS) for ss in ss_k]
    r_q = [lax.rsqrt(ss * (1.0 / QK_DIM) + EPS) * q_scale for ss in ss_q]
    k_rope = rope(kr * gk)
    q_rot = [rope(qh * gq) for qh in qhs]
    for h in range(N_HEADS):
        out = slice(h * LANES, (h + 1) * LANES)
        k_ref[:, out] = ((kns[h] * gk + k_rope) * r_k[h]).astype(BF16)
        q_ref[:, out] = (q_rot[h] * r_q[h]).astype(BF16)


def _rotary_block(w):
    first = w[..., HEAD_DIM:HEAD_DIM + ROPE_HALF]
    pad = jnp.zeros(w.shape[:-1] + (LANES - QK_DIM - ROPE_HALF,), w.dtype)
    return jnp.concatenate([w, first, pad], axis=-1)


def _pad_heads(w, width):
    k = w.shape[0]
    w = w.reshape(k, N_HEADS, width)
    w = jnp.pad(w, ((0, 0), (0, 0), (0, LANES - width)))
    return w.reshape(k, MLA_PAD)


def _mla_front(x, ln_g, w_in, qn_g, kvn_g, w_q_up, w_kv_up, q_head_g, k_head_g, cos_t, sin_t):
    tokens = x.shape[0]
    wcq = w_in[:, :Q_LORA].astype(BF16)
    wckv = w_in[:, Q_LORA:Q_LORA + KV_LORA].astype(BF16)
    wkr = w_in[:, Q_LORA + KV_LORA:Q_LORA + KV_LORA + ROPE_DIM]
    wkr = _rotary_block(jnp.pad(wkr, ((0, 0), (HEAD_DIM, 0)))).astype(BF16)
    wgate = w_in[:, Q_LORA + KV_LORA + ROPE_DIM:].astype(BF16)
    wq = _rotary_block(w_q_up.reshape(Q_LORA, N_HEADS, QK_DIM)).reshape(Q_LORA, MLA_PAD).astype(BF16)
    wkv = w_kv_up.reshape(KV_LORA, N_HEADS, 2 * HEAD_DIM)
    wk = _pad_heads(wkv[:, :, :HEAD_DIM].reshape(KV_LORA, MIX_WIDTH), HEAD_DIM).astype(BF16)
    wv = wkv[:, :, HEAD_DIM:].reshape(KV_LORA, MIX_WIDTH).astype(BF16)
    out = lambda w: jax.ShapeDtypeStruct((tokens, w), BF16)
    return pl.pallas_call(
        _mla_front_kernel,
        grid=(tokens // TOKEN_TILE,),
        in_specs=[_row_spec(D_MODEL), _const_spec((1, D_MODEL)),
                  _const_spec((D_MODEL, Q_LORA)), _const_spec((D_MODEL, KV_LORA)),
                  _const_spec((D_MODEL, LANES)), _const_spec((D_MODEL, MIX_WIDTH)),
                  _const_spec((1, Q_LORA)), _const_spec((1, KV_LORA)),
                  _const_spec((Q_LORA, MLA_PAD)), _const_spec((KV_LORA, MLA_PAD)),
                  _const_spec((KV_LORA, MIX_WIDTH)),
                  _const_spec((1, LANES)), _const_spec((1, LANES)), _const_spec((1, LANES)),
                  _row_spec(LANES), _row_spec(LANES)],
        out_specs=[_row_spec(MLA_PAD), _row_spec(MLA_PAD), _row_spec(MIX_WIDTH), _row_spec(MIX_WIDTH)],
        out_shape=[out(MLA_PAD), out(MLA_PAD), out(MIX_WIDTH), out(MIX_WIDTH)],
        compiler_params=_params(),
        name="mla_front",
    )(x, ln_g.reshape(1, D_MODEL), wcq, wckv, wkr, wgate, qn_g.reshape(1, Q_LORA),
      kvn_g.reshape(1, KV_LORA), wq, wk, wv, _rotary_block(q_head_g).reshape(1, LANES),
      _rotary_block(k_head_g).reshape(1, LANES), _keep_row(), cos_t, sin_t)


def _attn_params():
    return pltpu.CompilerParams(dimension_semantics=("arbitrary",) * 3, vmem_limit_bytes=VMEM_LIMIT)


def _first_head_lanes(shape):
    return lax.broadcasted_iota(jnp.int32, shape, 1) < HEAD_DIM


def _is_odd(i):
    return jnp.bitwise_and(i, 1) == 1


def _widen(x, width):
    return jnp.concatenate([x] * (width // LANES), axis=1)


def _mla_attn_kernel(q_ref, k_ref, v_ref, o_ref, m_scr, l_scr, acc_scr):
    qi = pl.program_id(2)
    n_wide = qi // (KEY_BLOCK // ATTN_TQ)
    row = lax.broadcasted_iota(jnp.int32, (ATTN_TQ, ATTN_TQ), 0)
    col = lax.broadcasted_iota(jnp.int32, (ATTN_TQ, ATTN_TQ), 1)
    causal = col <= row
    heads = range(MLA_HEADS_PER_STEP)
    qs = [q_ref[0, :, h * LANES:(h + 1) * LANES] for h in heads]

    def step(start, width, visible=None):
        scores = [_dot_nt(qs[h], k_ref[0, pl.ds(start, width), h * LANES:(h + 1) * LANES])
                  for h in heads]
        probs = []
        for h in heads:
            m = m_scr[h]
            s = scores[h] if visible is None else jnp.where(visible, scores[h], -jnp.inf)
            m_new = jnp.maximum(m, jnp.max(s, axis=-1, keepdims=True))
            p = jnp.exp2(s - _widen(m_new, width))
            alpha = jnp.exp2(m - m_new)
            l_scr[h] = alpha * l_scr[h] + jnp.sum(p, axis=-1, keepdims=True)
            m_scr[h] = m_new
            probs.append((alpha, p.astype(BF16)))
        for h in heads:
            alpha, p = probs[h]
            pair = h // HEADS_PER_BLOCK
            v = v_ref[0, pl.ds(start, width), pair * LANES:(pair + 1) * LANES]
            acc_scr[h] = alpha * acc_scr[h] + _dot(p, v)

    m_scr[...] = jnp.full(m_scr.shape, -jnp.inf, F32)
    l_scr[...] = jnp.zeros(l_scr.shape, F32)
    acc_scr[...] = jnp.zeros(acc_scr.shape, F32)

    @pl.loop(0, n_wide)
    def _(kb):
        step(pl.multiple_of(kb * KEY_BLOCK, KEY_BLOCK), KEY_BLOCK)

    @pl.when(_is_odd(qi))
    def _():
        step(pl.multiple_of(n_wide * KEY_BLOCK, KEY_BLOCK), ATTN_TQ)

    step(pl.multiple_of(qi * ATTN_TQ, ATTN_TQ), ATTN_TQ, causal)
    first = _first_head_lanes((ATTN_TQ, LANES))
    for b in range(MLA_HEADS_PER_STEP // HEADS_PER_BLOCK):
        o0 = acc_scr[2 * b] / l_scr[2 * b]
        o1 = acc_scr[2 * b + 1] / l_scr[2 * b + 1]
        o_ref[0, :, b * LANES:(b + 1) * LANES] = jnp.where(first, o0, o1).astype(BF16)


def _mla_attention(q, k, v, batch, seq):
    q = q.reshape(batch, seq, MLA_PAD)
    k = k.reshape(batch, seq, MLA_PAD)
    v = v.reshape(batch, seq, MIX_WIDTH)
    qk_w = MLA_HEADS_PER_STEP * LANES
    v_w = MLA_HEADS_PER_STEP * HEAD_DIM
    o = pl.pallas_call(
        _mla_attn_kernel,
        grid=(batch, N_HEADS // MLA_HEADS_PER_STEP, seq // ATTN_TQ),
        in_specs=[pl.BlockSpec((1, ATTN_TQ, qk_w), lambda b, g, i: (b, i, g)),
                  pl.BlockSpec((1, seq, qk_w), lambda b, g, i: (b, 0, g)),
                  pl.BlockSpec((1, seq, v_w), lambda b, g, i: (b, 0, g))],
        out_specs=pl.BlockSpec((1, ATTN_TQ, v_w), lambda b, g, i: (b, i, g)),
        out_shape=jax.ShapeDtypeStruct((batch, seq, MIX_WIDTH), BF16),
        scratch_shapes=[pltpu.VMEM((MLA_HEADS_PER_STEP, ATTN_TQ, LANES), F32)] * 3,
        compiler_params=_attn_params(),
        name="mla_attention",
    )(q, k, v)
    return o.reshape(batch * seq, MIX_WIDTH)


def _sb_attn_kernel(q_ref, k_ref, v_ref, o_ref, later_scr, acc_scr):
    qi = pl.program_id(2)
    t = ATTN_TQ
    row = lax.broadcasted_iota(jnp.int32, (t, t), 0)
    col = lax.broadcasted_iota(jnp.int32, (t, t), 1)
    strict = col < row
    suffix = jnp.where(row > col, 1.0, 0.0).astype(BF16)
    first = _first_head_lanes((t, LANES))
    q2 = q_ref[0]
    qs = [jnp.where(first, q2, jnp.zeros_like(q2)), jnp.where(first, jnp.zeros_like(q2), q2)]
    heads = range(HEADS_PER_BLOCK)

    def step(starts, visible=None):
        chains = [(h, start) for h in heads for start in starts]
        zs = [_dot_nt(qs[h], k_ref[0, pl.ds(start, t), :]) for h, start in chains]
        log_betas, log_rests = [], []
        for z in zs:
            soft = jnp.log(1.0 + jnp.exp2(-jnp.abs(z))) * LOG2_E
            log_beta = jnp.minimum(z, 0.0) - soft
            log_rest = log_beta - z
            if visible is not None:
                log_rest = jnp.where(visible, log_rest, 0.0)
            log_betas.append(log_beta)
            log_rests.append(log_rest)
        sums = [_dot(x.astype(BF16), suffix) for x in log_rests]
        laters = [later_scr[h] for h in heads]
        weights = []
        for i, (h, _) in enumerate(chains):
            a = jnp.exp2(log_betas[i] + sums[i] + _widen(laters[h], t))
            if visible is not None:
                a = jnp.where(visible, a, 0.0)
            weights.append(a.astype(BF16))
            laters[h] = laters[h] + jnp.sum(log_rests[i], axis=-1, keepdims=True)
        for h in heads:
            later_scr[h] = laters[h]
        for i, (h, start) in enumerate(chains):
            acc_scr[h] += _dot(weights[i], v_ref[0, pl.ds(start, t), :])

    later_scr[...] = jnp.zeros(later_scr.shape, F32)
    acc_scr[...] = jnp.zeros(acc_scr.shape, F32)
    diag = pl.multiple_of(qi * t, t)
    step([diag], strict)

    @pl.loop(0, qi // 2)
    def _(i):
        upper = pl.multiple_of(diag - (2 * i + 1) * t, t)
        step([upper, pl.multiple_of(upper - t, t)])

    @pl.when(_is_odd(qi))
    def _():
        step([0])

    o_ref[0] = jnp.where(first, acc_scr[0], acc_scr[1]).astype(BF16)


def _sb_attention(q, k, v, batch, seq):
    q = q.reshape(batch, seq, MIX_WIDTH)
    k = k.reshape(batch, seq, MIX_WIDTH)
    v = v.reshape(batch, seq, MIX_WIDTH)
    whole = pl.BlockSpec((1, seq, LANES), lambda b, p, i: (b, 0, p))
    tile = pl.BlockSpec((1, ATTN_TQ, LANES), lambda b, p, i: (b, i, p))
    o = pl.pallas_call(
        _sb_attn_kernel,
        grid=(batch, N_PAIRS, seq // ATTN_TQ),
        in_specs=[tile, whole, whole],
        out_specs=tile,
        out_shape=jax.ShapeDtypeStruct((batch, seq, MIX_WIDTH), BF16),
        scratch_shapes=[pltpu.VMEM((HEADS_PER_BLOCK, ATTN_TQ, LANES), F32)] * 2,
        compiler_params=_attn_params(),
        name="sb_attention",
    )(q, k, v)
    return o.reshape(batch * seq, MIX_WIDTH)


def _post_kernel(o_ref, gate_ref, x_ref, p_ref, wout_ref, wg_ref, wp_ref, *rest, with_kv):
    gate = gate_ref[...].astype(F32)
    u = o_ref[...].astype(F32) * (gate * _sigmoid(gate))
    y = x_ref[...] + _dot(u.astype(BF16), wout_ref[...])
    ple = _dot(p_ref[...].astype(BF16), wp_ref[...])
    x_new = y + _sigmoid(_dot(y.astype(BF16), wg_ref[...])) * ple
    if with_kv:
        kvg_ref, wkv_ref, xo_ref, k_ref, v_ref = rest
        kv = _dot(_rms(x_new, kvg_ref[...]).astype(BF16), wkv_ref[...])
        k_ref[...] = kv[:, :MIX_WIDTH].astype(BF16)
        v_ref[...] = kv[:, MIX_WIDTH:].astype(BF16)
    else:
        (xo_ref,) = rest
    xo_ref[...] = x_new


def _post(o, gate, x, p, w_out, w_gate, w_proj, kv_ln_g=None, w_kv=None):
    tokens = x.shape[0]
    with_kv = w_kv is not None
    in_specs = [_row_spec(MIX_WIDTH), _row_spec(MIX_WIDTH), _row_spec(D_MODEL), _row_spec(PLE_DIM),
                _const_spec((MIX_WIDTH, D_MODEL)), _const_spec((D_MODEL, D_MODEL)),
                _const_spec((PLE_DIM, D_MODEL))]
    args = [o, gate, x, p, w_out.astype(BF16), w_gate.astype(BF16), w_proj.astype(BF16)]
    out_specs = [_row_spec(D_MODEL)]
    out_shape = [jax.ShapeDtypeStruct((tokens, D_MODEL), F32)]
    if with_kv:
        in_specs += [_const_spec((1, D_MODEL)), _const_spec((D_MODEL, 2 * MIX_WIDTH))]
        args += [kv_ln_g.reshape(1, D_MODEL), w_kv.astype(BF16)]
        out_specs += [_row_spec(MIX_WIDTH), _row_spec(MIX_WIDTH)]
        out_shape += [jax.ShapeDtypeStruct((tokens, MIX_WIDTH), BF16)] * 2
    return pl.pallas_call(
        functools.partial(_post_kernel, with_kv=with_kv),
        grid=(tokens // TOKEN_TILE,),
        in_specs=in_specs, out_specs=out_specs, out_shape=out_shape,
        compiler_params=_params(),
        name="layer_tail_kv" if with_kv else "layer_tail",
    )(*args)


def _sb_front_kernel(x_ref, ln_ref, wq_ref, wgate_ref, q_ref, gate_ref):
    hb = _rms(x_ref[...], ln_ref[...]).astype(BF16)
    q_ref[...] = (_dot(hb, wq_ref[...]) * (HEAD_DIM ** -0.5 * LOG2_E)).astype(BF16)
    gate_ref[...] = _dot(hb, wgate_ref[...]).astype(BF16)


def _sb_front(x, ln_g, w_in):
    tokens = x.shape[0]
    out = jax.ShapeDtypeStruct((tokens, MIX_WIDTH), BF16)
    return pl.pallas_call(
        _sb_front_kernel,
        grid=(tokens // TOKEN_TILE,),
        in_specs=[_row_spec(D_MODEL), _const_spec((1, D_MODEL)),
                  _const_spec((D_MODEL, MIX_WIDTH)), _const_spec((D_MODEL, MIX_WIDTH))],
        out_specs=[_row_spec(MIX_WIDTH), _row_spec(MIX_WIDTH)],
        out_shape=[out, out],
        compiler_params=_params(),
        name="sb_front",
    )(x, ln_g.reshape(1, D_MODEL), w_in[:, :MIX_WIDTH].astype(BF16), w_in[:, MIX_WIDTH:].astype(BF16))


def kernel(x, p, positions, mla_ln_g, mla_w_in, mla_q_norm_g, mla_kv_norm_g, mla_w_q_up, mla_w_kv_up, mla_q_head_g, mla_k_head_g, mla_w_out, kv_ln_g, w_kv_shared, sb_ln_g, sb_w_in, sb_w_out, ple_w_proj, ple_w_gate):
    batch, seq, _ = x.shape
    tokens = batch * seq
    x = x.reshape(tokens, D_MODEL)
    p = p.reshape(DEPTH, tokens, PLE_DIM)
    cos_t, sin_t = _rope_tables(positions)
    k_sh = v_sh = None
    for i in range(DEPTH):
        if i < N_A:
            q, k, v, gate = _mla_front(x, mla_ln_g[i], mla_w_in[i], mla_q_norm_g[i], mla_kv_norm_g[i],
                                       mla_w_q_up[i], mla_w_kv_up[i], mla_q_head_g[i], mla_k_head_g[i],
                                       cos_t, sin_t)
            o = _mla_attention(q, k, v, batch, seq)
            w_out = mla_w_out[i]
        else:
            j = i - N_A
            q, gate = _sb_front(x, sb_ln_g[j], sb_w_in[j])
            o = _sb_attention(q, k_sh, v_sh, batch, seq)
            w_out = sb_w_out[j]
        if i == N_A - 1:
            x, k_sh, v_sh = _post(o, gate, x, p[i], w_out, ple_w_gate[i], ple_w_proj[i], kv_ln_g, w_kv_shared)
        else:
            (x,) = _post(o, gate, x, p[i], w_out, ple_w_gate[i], ple_w_proj[i])
    return x.reshape(batch, seq, D_MODEL)
```

```python
import functools

import jax
import jax.numpy as jnp
from jax import lax
from jax.experimental import pallas as pl
from jax.experimental.pallas import tpu as pltpu

D_MODEL = 1024
DEPTH = 4
N_A = DEPTH // 2
PLE_DIM = 256
N_HEADS = 16
HEAD_DIM = 64
ROPE_DIM = 32
ROPE_HALF = ROPE_DIM // 2
QK_DIM = HEAD_DIM + ROPE_DIM
Q_LORA = 384
KV_LORA = 256
MIX_WIDTH = N_HEADS * HEAD_DIM
ROPE_THETA = 10000.0
EPS = 1e-6

LANES = 128
HEADS_PER_BLOCK = LANES // HEAD_DIM
N_PAIRS = N_HEADS // HEADS_PER_BLOCK
MLA_PAD = N_HEADS * LANES
TOKEN_TILE = 512
ATTN_TQ = 256
KEY_BLOCK = 512
MLA_HEADS_PER_STEP = 8
SB_HEADS_PER_STEP = 4
SB_SKIP_BELOW = -200.0
LOG2_E = 1.4426950408889634
VMEM_LIMIT = 48 * 1024 * 1024

F32 = jnp.float32
BF16 = jnp.bfloat16


def _dot(a, b):
    return jnp.dot(a, b, preferred_element_type=F32)


def _dot_nt(a, b):
    return lax.dot_general(a, b, (((1,), (1,)), ((), ())), preferred_element_type=F32)


def _rms(x, g):
    return x * lax.rsqrt(jnp.mean(x * x, axis=-1, keepdims=True) + EPS) * g


def _sigmoid(x):
    return 1.0 / (1.0 + jnp.exp(-x))


def _params():
    return pltpu.CompilerParams(dimension_semantics=("arbitrary",), vmem_limit_bytes=VMEM_LIMIT)


def _const_spec(shape):
    return pl.BlockSpec(shape, lambda i: (0,) * len(shape))


def _row_spec(width, tile=TOKEN_TILE):
    return pl.BlockSpec((tile, width), lambda i: (i, 0))


def _rope_table_kernel(pos_ref, inv_ref, sign_ref, keep_ref, cos_ref, sin_ref):
    ang = pos_ref[...] * inv_ref[...]
    cos_ref[...] = jnp.cos(ang) * keep_ref[...]
    sin_ref[...] = jnp.sin(ang) * sign_ref[...]


def _head_block_row(nope, first, second):
    pad = jnp.zeros((LANES - QK_DIM - ROPE_HALF,), F32)
    return jnp.concatenate([nope, first, second, first, pad]).reshape(1, LANES)


def _rope_tables(positions):
    tokens = positions.size
    pos = positions.astype(F32).reshape(tokens, 1)
    inv = 1.0 / (ROPE_THETA ** (jnp.arange(ROPE_HALF, dtype=F32) / ROPE_HALF))
    zeros = jnp.zeros((HEAD_DIM,), F32)
    ones = jnp.ones((ROPE_HALF,), F32)
    inv_row = _head_block_row(zeros, inv, inv)
    sign_row = _head_block_row(zeros, -ones, ones) * _keep_row()
    out = jax.ShapeDtypeStruct((tokens, LANES), F32)
    return pl.pallas_call(
        _rope_table_kernel,
        grid=(tokens // TOKEN_TILE,),
        in_specs=[_row_spec(1), _const_spec((1, LANES)), _const_spec((1, LANES)), _const_spec((1, LANES))],
        out_specs=[_row_spec(LANES), _row_spec(LANES)],
        out_shape=[out, out],
        compiler_params=_params(),
        name="rope_tables",
    )(pos, inv_row, sign_row, _keep_row())


def _keep_row():
    return (jnp.arange(LANES) < QK_DIM).astype(F32).reshape(1, LANES)


def _mla_front_kernel(x_ref, ln_ref, wcq_ref, wckv_ref, wkr_ref, wgate_ref, qn_ref, kvn_ref,
                      wq_ref, wk_ref, wv_ref, gq_ref, gk_ref, keep_ref, cos_ref, sin_ref,
                      q_ref, k_ref, v_ref, gate_ref):
    hb = _rms(x_ref[...], ln_ref[...]).astype(BF16)
    cq = _dot(hb, wcq_ref[...])
    ckv = _dot(hb, wckv_ref[...])
    kr = _dot(hb, wkr_ref[...])
    gate_ref[...] = _dot(hb, wgate_ref[...]).astype(BF16)
    cqn = _rms(cq, qn_ref[...]).astype(BF16)
    ckvn = _rms(ckv, kvn_ref[...]).astype(BF16)
    v_ref[...] = _dot(ckvn, wv_ref[...]).astype(BF16)
    pair_cols = [slice(p * 2 * LANES, (p + 1) * 2 * LANES) for p in range(N_PAIRS)]
    kn2 = [_dot(ckvn, wk_ref[:, c]) for c in pair_cols]
    qh2 = [_dot(cqn, wq_ref[:, c]) for c in pair_cols]
    head_blocks = [(p, slice(j * LANES, (j + 1) * LANES))
                   for p in range(N_PAIRS) for j in range(HEADS_PER_BLOCK)]
    kns = [kn2[p][:, blk] for p, blk in head_blocks]
    qhs = [qh2[p][:, blk] for p, blk in head_blocks]

    keep = keep_ref[...]
    cos = cos_ref[...]
    sin = sin_ref[...]
    gq = gq_ref[...]
    gk = gk_ref[...]

    def rope(y):
        return y * cos + pltpu.roll(y, LANES - ROPE_HALF, 1) * sin

    ss_kr = jnp.sum(kr * kr * keep, axis=-1, keepdims=True)
    ss_k = [jnp.sum(kn * kn, axis=-1, keepdims=True) for kn in kns]
    ss_q = [jnp.sum(qh * qh * keep, axis=-1, keepdims=True) for qh in qhs]
    q_scale = QK_DIM ** -0.5 * LOG2_E
    r_k = [lax.rsqrt((ss + ss_kr) * (1.0 / QK_DIM) + EPS) for ss in ss_k]
    r_q = [lax.rsqrt(ss * (1.0 / QK_DIM) + EPS) * q_scale for ss in ss_q]
    k_rope = rope(kr * gk)
    q_rot = [rope(qh * gq) for qh in qhs]
    for h in range(N_HEADS):
        out = slice(h * LANES, (h + 1) * LANES)
        k_ref[:, out] = ((kns[h] * gk + k_rope) * r_k[h]).astype(BF16)
        q_ref[:, out] = (q_rot[h] * r_q[h]).astype(BF16)


def _rotary_block(w):
    first = w[..., HEAD_DIM:HEAD_DIM + ROPE_HALF]
    pad = jnp.zeros(w.shape[:-1] + (LANES - QK_DIM - ROPE_HALF,), w.dtype)
    return jnp.concatenate([w, first, pad], axis=-1)


def _pad_heads(w, width):
    k = w.shape[0]
    w = w.reshape(k, N_HEADS, width)
    w = jnp.pad(w, ((0, 0), (0, 0), (0, LANES - width)))
    return w.reshape(k, MLA_PAD)


def _mla_front(x, ln_g, w_in, qn_g, kvn_g, w_q_up, w_kv_up, q_head_g, k_head_g, cos_t, sin_t):
    tokens = x.shape[0]
    wcq = w_in[:, :Q_LORA].astype(BF16)
    wckv = w_in[:, Q_LORA:Q_LORA + KV_LORA].astype(BF16)
    wkr = w_in[:, Q_LORA + KV_LORA:Q_LORA + KV_LORA + ROPE_DIM]
    wkr = _rotary_block(jnp.pad(wkr, ((0, 0), (HEAD_DIM, 0)))).astype(BF16)
    wgate = w_in[:, Q_LORA + KV_LORA + ROPE_DIM:].astype(BF16)
    wq = _rotary_block(w_q_up.reshape(Q_LORA, N_HEADS, QK_DIM)).reshape(Q_LORA, MLA_PAD).astype(BF16)
    wkv = w_kv_up.reshape(KV_LORA, N_HEADS, 2 * HEAD_DIM)
    wk = _pad_heads(wkv[:, :, :HEAD_DIM].reshape(KV_LORA, MIX_WIDTH), HEAD_DIM).astype(BF16)
    wv = wkv[:, :, HEAD_DIM:].reshape(KV_LORA, MIX_WIDTH).astype(BF16)
    out = lambda w: jax.ShapeDtypeStruct((tokens, w), BF16)
    return pl.pallas_call(
        _mla_front_kernel,
        grid=(tokens // TOKEN_TILE,),
        in_specs=[_row_spec(D_MODEL), _const_spec((1, D_MODEL)),
                  _const_spec((D_MODEL, Q_LORA)), _const_spec((D_MODEL, KV_LORA)),
                  _const_spec((D_MODEL, LANES)), _const_spec((D_MODEL, MIX_WIDTH)),
                  _const_spec((1, Q_LORA)), _const_spec((1, KV_LORA)),
                  _const_spec((Q_LORA, MLA_PAD)), _const_spec((KV_LORA, MLA_PAD)),
                  _const_spec((KV_LORA, MIX_WIDTH)),
                  _const_spec((1, LANES)), _const_spec((1, LANES)), _const_spec((1, LANES)),
                  _row_spec(LANES), _row_spec(LANES)],
        out_specs=[_row_spec(MLA_PAD), _row_spec(MLA_PAD), _row_spec(MIX_WIDTH), _row_spec(MIX_WIDTH)],
        out_shape=[out(MLA_PAD), out(MLA_PAD), out(MIX_WIDTH), out(MIX_WIDTH)],
        compiler_params=_params(),
        name="mla_front",
    )(x, ln_g.reshape(1, D_MODEL), wcq, wckv, wkr, wgate, qn_g.reshape(1, Q_LORA),
      kvn_g.reshape(1, KV_LORA), wq, wk, wv, _rotary_block(q_head_g).reshape(1, LANES),
      _rotary_block(k_head_g).reshape(1, LANES), _keep_row(), cos_t, sin_t)


def _attn_params():
    return pltpu.CompilerParams(dimension_semantics=("arbitrary",) * 3, vmem_limit_bytes=VMEM_LIMIT)


def _first_head_lanes(shape):
    return lax.broadcasted_iota(jnp.int32, shape, 1) < HEAD_DIM


def _is_odd(i):
    return jnp.bitwise_and(i, 1) == 1


def _widen(x, width):
    return jnp.concatenate([x] * (width // LANES), axis=1)


def _mla_attn_kernel(q_ref, k_ref, v_ref, o_ref, m_scr, l_scr, acc_scr):
    qi = pl.program_id(2)
    n_wide = qi // (KEY_BLOCK // ATTN_TQ)
    row = lax.broadcasted_iota(jnp.int32, (ATTN_TQ, ATTN_TQ), 0)
    col = lax.broadcasted_iota(jnp.int32, (ATTN_TQ, ATTN_TQ), 1)
    causal = col <= row
    heads = range(MLA_HEADS_PER_STEP)
    qs = [q_ref[0, :, h * LANES:(h + 1) * LANES] for h in heads]

    def step(start, width, visible=None, fresh=False):
        scores = [_dot_nt(qs[h], k_ref[0, pl.ds(start, width), h * LANES:(h + 1) * LANES])
                  for h in heads]
        probs = []
        for h in heads:
            s = scores[h] if visible is None else jnp.where(visible, scores[h], -jnp.inf)
            block_max = jnp.broadcast_to(jnp.max(s, axis=-1, keepdims=True), (ATTN_TQ, LANES))
            m_new = block_max if fresh else jnp.maximum(m_scr[h], block_max)
            p = jnp.exp2(s - _widen(m_new, width))
            block_sum = jnp.broadcast_to(jnp.sum(p, axis=-1, keepdims=True), (ATTN_TQ, LANES))
            alpha = None if fresh else jnp.exp2(m_scr[h] - m_new)
            l_scr[h] = block_sum if fresh else alpha * l_scr[h] + block_sum
            m_scr[h] = m_new
            probs.append((alpha, p.astype(BF16)))
        for h in heads:
            alpha, p = probs[h]
            pair = h // HEADS_PER_BLOCK
            pv = _dot(p, v_ref[0, pl.ds(start, width), pair * LANES:(pair + 1) * LANES])
            acc_scr[h] = pv if fresh else alpha * acc_scr[h] + pv

    step(pl.multiple_of(qi * ATTN_TQ, ATTN_TQ), ATTN_TQ, causal, fresh=True)

    @pl.loop(0, n_wide)
    def _(kb):
        step(pl.multiple_of(kb * KEY_BLOCK, KEY_BLOCK), KEY_BLOCK)

    @pl.when(_is_odd(qi))
    def _():
        step(pl.multiple_of(n_wide * KEY_BLOCK, KEY_BLOCK), ATTN_TQ)

    first = _first_head_lanes((ATTN_TQ, LANES))
    for b in range(MLA_HEADS_PER_STEP // HEADS_PER_BLOCK):
        o0 = acc_scr[2 * b] / l_scr[2 * b]
        o1 = acc_scr[2 * b + 1] / l_scr[2 * b + 1]
        o_ref[0, :, b * LANES:(b + 1) * LANES] = jnp.where(first, o0, o1).astype(BF16)


def _mla_attention(q, k, v, batch, seq):
    q = q.reshape(batch, seq, MLA_PAD)
    k = k.reshape(batch, seq, MLA_PAD)
    v = v.reshape(batch, seq, MIX_WIDTH)
    qk_w = MLA_HEADS_PER_STEP * LANES
    v_w = MLA_HEADS_PER_STEP * HEAD_DIM
    o = pl.pallas_call(
        _mla_attn_kernel,
        grid=(batch, N_HEADS // MLA_HEADS_PER_STEP, seq // ATTN_TQ),
        in_specs=[pl.BlockSpec((1, ATTN_TQ, qk_w), lambda b, g, i: (b, i, g)),
                  pl.BlockSpec((1, seq, qk_w), lambda b, g, i: (b, 0, g)),
                  pl.BlockSpec((1, seq, v_w), lambda b, g, i: (b, 0, g))],
        out_specs=pl.BlockSpec((1, ATTN_TQ, v_w), lambda b, g, i: (b, i, g)),
        out_shape=jax.ShapeDtypeStruct((batch, seq, MIX_WIDTH), BF16),
        scratch_shapes=[pltpu.VMEM((MLA_HEADS_PER_STEP, ATTN_TQ, LANES), F32)] * 3,
        compiler_params=_attn_params(),
        name="mla_attention",
    )(q, k, v)
    return o.reshape(batch * seq, MIX_WIDTH)


def _sb_attn_kernel(q_ref, k_ref, v_ref, o_ref, later_scr, acc_scr):
    qi = pl.program_id(2)
    t = ATTN_TQ
    row = lax.broadcasted_iota(jnp.int32, (t, t), 0)
    col = lax.broadcasted_iota(jnp.int32, (t, t), 1)
    strict = col < row
    suffix = jnp.where(row > col, 1.0, 0.0).astype(BF16)
    first = _first_head_lanes((t, LANES))
    heads = range(SB_HEADS_PER_STEP)
    lanes = [slice(h // HEADS_PER_BLOCK * LANES, (h // HEADS_PER_BLOCK + 1) * LANES) for h in heads]
    qs = []
    for h in heads:
        q2 = q_ref[0, :, lanes[h]]
        mine = first if h % HEADS_PER_BLOCK == 0 else jnp.logical_not(first)
        qs.append(jnp.where(mine, q2, jnp.zeros_like(q2)))

    def step(lowest, n_blocks, visible=None, fresh=False):
        chains = [(h, j) for h in heads for j in reversed(range(n_blocks))]
        zs = [_dot_nt(qs[h], k_ref[0, pl.ds(lowest + j * t, t), lanes[h]]) for h, j in chains]
        log_betas, log_rests = [], []
        for z in zs:
            soft = jnp.log(1.0 + jnp.exp2(-jnp.abs(z))) * LOG2_E
            log_beta = jnp.minimum(z, 0.0) - soft
            log_rest = log_beta - z
            if visible is not None:
                log_rest = jnp.where(visible, log_rest, 0.0)
            log_betas.append(log_beta)
            log_rests.append(log_rest)
        sums = [_dot(x.astype(BF16), suffix) for x in log_rests]
        laters = [None if fresh else later_scr[h] for h in heads]
        weights = {}
        for i, (h, j) in enumerate(chains):
            x = log_betas[i] + sums[i]
            if laters[h] is not None:
                x = x + _widen(laters[h], t)
            a = jnp.exp2(x)
            if visible is not None:
                a = jnp.where(visible, a, 0.0)
            weights[h, j] = a.astype(BF16)
            block_sum = jnp.broadcast_to(jnp.sum(log_rests[i], axis=-1, keepdims=True), (t, LANES))
            laters[h] = block_sum if laters[h] is None else laters[h] + block_sum
        for h in heads:
            later_scr[h] = laters[h]
            w = jnp.concatenate([weights[h, j] for j in range(n_blocks)], axis=1)
            pv = _dot(w, v_ref[0, pl.ds(lowest, n_blocks * t), lanes[h]])
            acc_scr[h] = pv if fresh else acc_scr[h] + pv
        return functools.reduce(jnp.maximum, [jnp.max(x) for x in laters])

    diag = pl.multiple_of(qi * t, t)
    highest_later = step(diag, 1, strict, fresh=True)

    def more(c):
        blocks_done, highest = c
        return jnp.logical_and(blocks_done < qi, highest > SB_SKIP_BELOW)

    def one_block(c):
        blocks_done, _ = c
        return blocks_done + 1, step(pl.multiple_of(diag - (blocks_done + 1) * t, t), 1)

    lax.while_loop(more, one_block, (jnp.int32(0), highest_later))

    for b in range(SB_HEADS_PER_STEP // HEADS_PER_BLOCK):
        o_ref[0, :, b * LANES:(b + 1) * LANES] = jnp.where(
            first, acc_scr[HEADS_PER_BLOCK * b], acc_scr[HEADS_PER_BLOCK * b + 1]).astype(BF16)


def _sb_attention(q, k, v, batch, seq):
    q = q.reshape(batch, seq, MIX_WIDTH)
    k = k.reshape(batch, seq, MIX_WIDTH)
    v = v.reshape(batch, seq, MIX_WIDTH)
    width = SB_HEADS_PER_STEP * HEAD_DIM
    whole = pl.BlockSpec((1, seq, width), lambda b, g, i: (b, 0, g))
    tile = pl.BlockSpec((1, ATTN_TQ, width), lambda b, g, i: (b, i, g))
    o = pl.pallas_call(
        _sb_attn_kernel,
        grid=(batch, N_HEADS // SB_HEADS_PER_STEP, seq // ATTN_TQ),
        in_specs=[tile, whole, whole],
        out_specs=tile,
        out_shape=jax.ShapeDtypeStruct((batch, seq, MIX_WIDTH), BF16),
        scratch_shapes=[pltpu.VMEM((SB_HEADS_PER_STEP, ATTN_TQ, LANES), F32)] * 2,
        compiler_params=_attn_params(),
        name="sb_attention",
    )(q, k, v)
    return o.reshape(batch * seq, MIX_WIDTH)


def _post_kernel(o_ref, gate_ref, x_ref, p_ref, wout_ref, wg_ref, wp_ref, *rest, with_kv):
    gate = gate_ref[...].astype(F32)
    u = o_ref[...].astype(F32) * (gate * _sigmoid(gate))
    y = x_ref[...] + _dot(u.astype(BF16), wout_ref[...])
    ple = _dot(p_ref[...].astype(BF16), wp_ref[...])
    x_new = y + _sigmoid(_dot(y.astype(BF16), wg_ref[...])) * ple
    if with_kv:
        kvg_ref, wkv_ref, xo_ref, k_ref, v_ref = rest
        kv = _dot(_rms(x_new, kvg_ref[...]).astype(BF16), wkv_ref[...])
        k_ref[...] = kv[:, :MIX_WIDTH].astype(BF16)
        v_ref[...] = kv[:, MIX_WIDTH:].astype(BF16)
    else:
        (xo_ref,) = rest
    xo_ref[...] = x_new


def _post(o, gate, x, p, w_out, w_gate, w_proj, kv_ln_g=None, w_kv=None):
    tokens = x.shape[0]
    with_kv = w_kv is not None
    in_specs = [_row_spec(MIX_WIDTH), _row_spec(MIX_WIDTH), _row_spec(D_MODEL), _row_spec(PLE_DIM),
                _const_spec((MIX_WIDTH, D_MODEL)), _const_spec((D_MODEL, D_MODEL)),
                _const_spec((PLE_DIM, D_MODEL))]
    args = [o, gate, x, p, w_out.astype(BF16), w_gate.astype(BF16), w_proj.astype(BF16)]
    out_specs = [_row_spec(D_MODEL)]
    out_shape = [jax.ShapeDtypeStruct((tokens, D_MODEL), F32)]
    if with_kv:
        in_specs += [_const_spec((1, D_MODEL)), _const_spec((D_MODEL, 2 * MIX_WIDTH))]
        args += [kv_ln_g.reshape(1, D_MODEL), w_kv.astype(BF16)]
        out_specs += [_row_spec(MIX_WIDTH), _row_spec(MIX_WIDTH)]
        out_shape += [jax.ShapeDtypeStruct((tokens, MIX_WIDTH), BF16)] * 2
    return pl.pallas_call(
        functools.partial(_post_kernel, with_kv=with_kv),
        grid=(tokens // TOKEN_TILE,),
        in_specs=in_specs, out_specs=out_specs, out_shape=out_shape,
        compiler_params=_params(),
        name="layer_tail_kv" if with_kv else "layer_tail",
    )(*args)


def _sb_front_kernel(x_ref, ln_ref, wq_ref, wgate_ref, q_ref, gate_ref):
    hb = _rms(x_ref[...], ln_ref[...]).astype(BF16)
    q_ref[...] = (_dot(hb, wq_ref[...]) * (HEAD_DIM ** -0.5 * LOG2_E)).astype(BF16)
    gate_ref[...] = _dot(hb, wgate_ref[...]).astype(BF16)


def _sb_front(x, ln_g, w_in):
    tokens = x.shape[0]
    out = jax.ShapeDtypeStruct((tokens, MIX_WIDTH), BF16)
    return pl.pallas_call(
        _sb_front_kernel,
        grid=(tokens // TOKEN_TILE,),
        in_specs=[_row_spec(D_MODEL), _const_spec((1, D_MODEL)),
                  _const_spec((D_MODEL, MIX_WIDTH)), _const_spec((D_MODEL, MIX_WIDTH))],
        out_specs=[_row_spec(MIX_WIDTH), _row_spec(MIX_WIDTH)],
        out_shape=[out, out],
        compiler_params=_params(),
        name="sb_front",
    )(x, ln_g.reshape(1, D_MODEL), w_in[:, :MIX_WIDTH].astype(BF16), w_in[:, MIX_WIDTH:].astype(BF16))


def kernel(x, p, positions, mla_ln_g, mla_w_in, mla_q_norm_g, mla_kv_norm_g, mla_w_q_up, mla_w_kv_up, mla_q_head_g, mla_k_head_g, mla_w_out, kv_ln_g, w_kv_shared, sb_ln_g, sb_w_in, sb_w_out, ple_w_proj, ple_w_gate):
    batch, seq, _ = x.shape
    tokens = batch * seq
    x = x.reshape(tokens, D_MODEL)
    p = p.reshape(DEPTH, tokens, PLE_DIM)
    cos_t, sin_t = _rope_tables(positions)
    k_sh = v_sh = None
    for i in range(DEPTH):
        if i < N_A:
            q, k, v, gate = _mla_front(x, mla_ln_g[i], mla_w_in[i], mla_q_norm_g[i], mla_kv_norm_g[i],
                                       mla_w_q_up[i], mla_w_kv_up[i], mla_q_head_g[i], mla_k_head_g[i],
                                       cos_t, sin_t)
            o = _mla_attention(q, k, v, batch, seq)
            w_out = mla_w_out[i]
        else:
            j = i - N_A
            q, gate = _sb_front(x, sb_ln_g[j], sb_w_in[j])
            o = _sb_attention(q, k_sh, v_sh, batch, seq)
            w_out = sb_w_out[j]
        if i == N_A - 1:
            x, k_sh, v_sh = _post(o, gate, x, p[i], w_out, ple_w_gate[i], ple_w_proj[i], kv_ln_g, w_kv_shared)
        else:
            (x,) = _post(o, gate, x, p[i], w_out, ple_w_gate[i], ple_w_proj[i])
    return x.reshape(batch, seq, D_MODEL)
```

```python
import functools

import jax
import jax.numpy as jnp
from jax import lax
from jax.experimental import pallas as pl
from jax.experimental.pallas import tpu as pltpu

D_MODEL = 1024
DEPTH = 4
N_A = DEPTH // 2
PLE_DIM = 256
N_HEADS = 16
HEAD_DIM = 64
ROPE_DIM = 32
ROPE_HALF = ROPE_DIM // 2
QK_DIM = HEAD_DIM + ROPE_DIM
Q_LORA = 384
KV_LORA = 256
MIX_WIDTH = N_HEADS * HEAD_DIM
ROPE_THETA = 10000.0
EPS = 1e-6

LANES = 128
HEADS_PER_BLOCK = LANES // HEAD_DIM
N_PAIRS = N_HEADS // HEADS_PER_BLOCK
MLA_PAD = N_HEADS * LANES
TOKEN_TILE = 512
FRONT_ROWS = 256
ATTN_TQ = 256
KEY_BLOCK = 512
MLA_HEADS_PER_STEP = 8
SB_HEADS_PER_STEP = 8
SB_SKIP_BELOW = -200.0
LOG2_E = 1.4426950408889634
VMEM_LIMIT = 48 * 1024 * 1024

F32 = jnp.float32
BF16 = jnp.bfloat16


def _dot(a, b):
    return jnp.dot(a, b, preferred_element_type=F32)


def _dot_nt(a, b):
    return lax.dot_general(a, b, (((1,), (1,)), ((), ())), preferred_element_type=F32)


def _rms(x, g):
    return x * lax.rsqrt(jnp.mean(x * x, axis=-1, keepdims=True) + EPS) * g


def _sigmoid(x):
    return 1.0 / (1.0 + jnp.exp(-x))


def _params():
    return pltpu.CompilerParams(dimension_semantics=("arbitrary",), vmem_limit_bytes=VMEM_LIMIT)


def _const_spec(shape):
    return pl.BlockSpec(shape, lambda i: (0,) * len(shape))


def _row_spec(width, tile=TOKEN_TILE):
    return pl.BlockSpec((tile, width), lambda i: (i, 0))


def _rope_table_kernel(pos_ref, inv_ref, sign_ref, keep_ref, cos_ref, sin_ref):
    ang = pos_ref[...] * inv_ref[...]
    cos_ref[...] = jnp.cos(ang) * keep_ref[...]
    sin_ref[...] = jnp.sin(ang) * sign_ref[...]


def _head_block_row(nope, first, second):
    pad = jnp.zeros((LANES - QK_DIM - ROPE_HALF,), F32)
    return jnp.concatenate([nope, first, second, first, pad]).reshape(1, LANES)


def _rope_tables(positions):
    tokens = positions.size
    pos = positions.astype(F32).reshape(tokens, 1)
    inv = 1.0 / (ROPE_THETA ** (jnp.arange(ROPE_HALF, dtype=F32) / ROPE_HALF))
    zeros = jnp.zeros((HEAD_DIM,), F32)
    ones = jnp.ones((ROPE_HALF,), F32)
    inv_row = _head_block_row(zeros, inv, inv)
    sign_row = _head_block_row(zeros, -ones, ones) * _keep_row()
    out = jax.ShapeDtypeStruct((tokens, LANES), F32)
    return pl.pallas_call(
        _rope_table_kernel,
        grid=(tokens // TOKEN_TILE,),
        in_specs=[_row_spec(1), _const_spec((1, LANES)), _const_spec((1, LANES)), _const_spec((1, LANES))],
        out_specs=[_row_spec(LANES), _row_spec(LANES)],
        out_shape=[out, out],
        compiler_params=_params(),
        name="rope_tables",
    )(pos, inv_row, sign_row, _keep_row())


def _keep_row():
    return (jnp.arange(LANES) < QK_DIM).astype(F32).reshape(1, LANES)


def _mla_front_kernel(x_ref, ln_ref, wcq_ref, wckv_ref, wkr_ref, wgate_ref, qn_ref, kvn_ref,
                      wq_ref, wk_ref, wv_ref, gq_ref, gk_ref, keep_ref, cos_ref, sin_ref,
                      q_ref, k_ref, v_ref, gate_ref):
    keep = keep_ref[...]
    gq = gq_ref[...]
    gk = gk_ref[...]
    pair_cols = [slice(p * 2 * LANES, (p + 1) * 2 * LANES) for p in range(N_PAIRS)]
    head_blocks = [(p, slice(j * LANES, (j + 1) * LANES))
                   for p in range(N_PAIRS) for j in range(HEADS_PER_BLOCK)]

    def project(rows):
        hb = _rms(x_ref[rows, :], ln_ref[...]).astype(BF16)
        cq = _dot(hb, wcq_ref[...])
        ckv = _dot(hb, wckv_ref[...])
        kr = _dot(hb, wkr_ref[...])
        gate_ref[rows, :] = _dot(hb, wgate_ref[...]).astype(BF16)
        cqn = _rms(cq, qn_ref[...]).astype(BF16)
        ckvn = _rms(ckv, kvn_ref[...]).astype(BF16)
        v_ref[rows, :] = _dot(ckvn, wv_ref[...]).astype(BF16)
        kn2 = [_dot(ckvn, wk_ref[:, c]) for c in pair_cols]
        qh2 = [_dot(cqn, wq_ref[:, c]) for c in pair_cols]
        return kr, kn2, qh2

    def finish(rows, kr, kn2, qh2):
        cos = cos_ref[rows, :]
        sin = sin_ref[rows, :]

        def rope(y):
            return y * cos + pltpu.roll(y, LANES - ROPE_HALF, 1) * sin

        kns = [kn2[p][:, blk] for p, blk in head_blocks]
        qhs = [qh2[p][:, blk] for p, blk in head_blocks]
        ss_kr = jnp.sum(kr * kr * keep, axis=-1, keepdims=True)
        ss_k = [jnp.sum(kn * kn, axis=-1, keepdims=True) for kn in kns]
        ss_q = [jnp.sum(qh * qh * keep, axis=-1, keepdims=True) for qh in qhs]
        q_scale = QK_DIM ** -0.5 * LOG2_E
        r_k = [lax.rsqrt((ss + ss_kr) * (1.0 / QK_DIM) + EPS) for ss in ss_k]
        r_q = [lax.rsqrt(ss * (1.0 / QK_DIM) + EPS) * q_scale for ss in ss_q]
        k_rope = rope(kr * gk)
        q_rot = [rope(qh * gq) for qh in qhs]
        for h in range(N_HEADS):
            out = slice(h * LANES, (h + 1) * LANES)
            k_ref[rows, out] = ((kns[h] * gk + k_rope) * r_k[h]).astype(BF16)
            q_ref[rows, out] = (q_rot[h] * r_q[h]).astype(BF16)

    groups = [slice(r, r + FRONT_ROWS) for r in range(0, TOKEN_TILE, FRONT_ROWS)]
    projected = [project(rows) for rows in groups]
    for rows, raw in zip(groups, projected):
        finish(rows, *raw)


def _rotary_block(w):
    first = w[..., HEAD_DIM:HEAD_DIM + ROPE_HALF]
    pad = jnp.zeros(w.shape[:-1] + (LANES - QK_DIM - ROPE_HALF,), w.dtype)
    return jnp.concatenate([w, first, pad], axis=-1)


def _pad_heads(w, width):
    k = w.shape[0]
    w = w.reshape(k, N_HEADS, width)
    w = jnp.pad(w, ((0, 0), (0, 0), (0, LANES - width)))
    return w.reshape(k, MLA_PAD)


def _mla_front(x, ln_g, w_in, qn_g, kvn_g, w_q_up, w_kv_up, q_head_g, k_head_g, cos_t, sin_t):
    tokens = x.shape[0]
    wcq = w_in[:, :Q_LORA].astype(BF16)
    wckv = w_in[:, Q_LORA:Q_LORA + KV_LORA].astype(BF16)
    wkr = w_in[:, Q_LORA + KV_LORA:Q_LORA + KV_LORA + ROPE_DIM]
    wkr = _rotary_block(jnp.pad(wkr, ((0, 0), (HEAD_DIM, 0)))).astype(BF16)
    wgate = w_in[:, Q_LORA + KV_LORA + ROPE_DIM:].astype(BF16)
    wq = _rotary_block(w_q_up.reshape(Q_LORA, N_HEADS, QK_DIM)).reshape(Q_LORA, MLA_PAD).astype(BF16)
    wkv = w_kv_up.reshape(KV_LORA, N_HEADS, 2 * HEAD_DIM)
    wk = _pad_heads(wkv[:, :, :HEAD_DIM].reshape(KV_LORA, MIX_WIDTH), HEAD_DIM).astype(BF16)
    wv = wkv[:, :, HEAD_DIM:].reshape(KV_LORA, MIX_WIDTH).astype(BF16)
    out = lambda w: jax.ShapeDtypeStruct((tokens, w), BF16)
    return pl.pallas_call(
        _mla_front_kernel,
        grid=(tokens // TOKEN_TILE,),
        in_specs=[_row_spec(D_MODEL), _const_spec((1, D_MODEL)),
                  _const_spec((D_MODEL, Q_LORA)), _const_spec((D_MODEL, KV_LORA)),
                  _const_spec((D_MODEL, LANES)), _const_spec((D_MODEL, MIX_WIDTH)),
                  _const_spec((1, Q_LORA)), _const_spec((1, KV_LORA)),
                  _const_spec((Q_LORA, MLA_PAD)), _const_spec((KV_LORA, MLA_PAD)),
                  _const_spec((KV_LORA, MIX_WIDTH)),
                  _const_spec((1, LANES)), _const_spec((1, LANES)), _const_spec((1, LANES)),
                  _row_spec(LANES), _row_spec(LANES)],
        out_specs=[_row_spec(MLA_PAD), _row_spec(MLA_PAD), _row_spec(MIX_WIDTH), _row_spec(MIX_WIDTH)],
        out_shape=[out(MLA_PAD), out(MLA_PAD), out(MIX_WIDTH), out(MIX_WIDTH)],
        compiler_params=_params(),
        name="mla_front",
    )(x, ln_g.reshape(1, D_MODEL), wcq, wckv, wkr, wgate, qn_g.reshape(1, Q_LORA),
      kvn_g.reshape(1, KV_LORA), wq, wk, wv, _rotary_block(q_head_g).reshape(1, LANES),
      _rotary_block(k_head_g).reshape(1, LANES), _keep_row(), cos_t, sin_t)


def _attn_params():
    return pltpu.CompilerParams(dimension_semantics=("arbitrary",) * 3, vmem_limit_bytes=VMEM_LIMIT)


def _first_head_lanes(shape):
    return lax.broadcasted_iota(jnp.int32, shape, 1) < HEAD_DIM


def _is_odd(i):
    return jnp.bitwise_and(i, 1) == 1


def _widen(x, width):
    return jnp.concatenate([x] * (width // LANES), axis=1)


def _mla_attn_kernel(q_ref, k_ref, v_ref, o_ref, m_scr, l_scr, acc_scr):
    qi = pl.program_id(2)
    n_wide = qi // (KEY_BLOCK // ATTN_TQ)
    row = lax.broadcasted_iota(jnp.int32, (ATTN_TQ, ATTN_TQ), 0)
    col = lax.broadcasted_iota(jnp.int32, (ATTN_TQ, ATTN_TQ), 1)
    causal = col <= row
    heads = range(MLA_HEADS_PER_STEP)
    qs = [q_ref[0, :, h * LANES:(h + 1) * LANES] for h in heads]

    def step(start, width, visible=None, fresh=False):
        scores = [_dot_nt(qs[h], k_ref[0, pl.ds(start, width), h * LANES:(h + 1) * LANES])
                  for h in heads]
        probs = []
        for h in heads:
            s = scores[h] if visible is None else jnp.where(visible, scores[h], -jnp.inf)
            block_max = jnp.broadcast_to(jnp.max(s, axis=-1, keepdims=True), (ATTN_TQ, LANES))
            m_new = block_max if fresh else jnp.maximum(m_scr[h], block_max)
            p = jnp.exp2(s - _widen(m_new, width))
            block_sum = jnp.broadcast_to(jnp.sum(p, axis=-1, keepdims=True), (ATTN_TQ, LANES))
            alpha = None if fresh else jnp.exp2(m_scr[h] - m_new)
            l_scr[h] = block_sum if fresh else alpha * l_scr[h] + block_sum
            m_scr[h] = m_new
            probs.append((alpha, p.astype(BF16)))
        for h in heads:
            alpha, p = probs[h]
            pair = h // HEADS_PER_BLOCK
            pv = _dot(p, v_ref[0, pl.ds(start, width), pair * LANES:(pair + 1) * LANES])
            acc_scr[h] = pv if fresh else alpha * acc_scr[h] + pv

    step(pl.multiple_of(qi * ATTN_TQ, ATTN_TQ), ATTN_TQ, causal, fresh=True)

    @pl.loop(0, n_wide)
    def _(kb):
        step(pl.multiple_of(kb * KEY_BLOCK, KEY_BLOCK), KEY_BLOCK)

    @pl.when(_is_odd(qi))
    def _():
        step(pl.multiple_of(n_wide * KEY_BLOCK, KEY_BLOCK), ATTN_TQ)

    first = _first_head_lanes((ATTN_TQ, LANES))
    for b in range(MLA_HEADS_PER_STEP // HEADS_PER_BLOCK):
        o0 = acc_scr[2 * b] / l_scr[2 * b]
        o1 = acc_scr[2 * b + 1] / l_scr[2 * b + 1]
        o_ref[0, :, b * LANES:(b + 1) * LANES] = jnp.where(first, o0, o1).astype(BF16)


def _mla_attention(q, k, v, batch, seq):
    q = q.reshape(batch, seq, MLA_PAD)
    k = k.reshape(batch, seq, MLA_PAD)
    v = v.reshape(batch, seq, MIX_WIDTH)
    qk_w = MLA_HEADS_PER_STEP * LANES
    v_w = MLA_HEADS_PER_STEP * HEAD_DIM
    o = pl.pallas_call(
        _mla_attn_kernel,
        grid=(batch, N_HEADS // MLA_HEADS_PER_STEP, seq // ATTN_TQ),
        in_specs=[pl.BlockSpec((1, ATTN_TQ, qk_w), lambda b, g, i: (b, i, g)),
                  pl.BlockSpec((1, seq, qk_w), lambda b, g, i: (b, 0, g)),
                  pl.BlockSpec((1, seq, v_w), lambda b, g, i: (b, 0, g))],
        out_specs=pl.BlockSpec((1, ATTN_TQ, v_w), lambda b, g, i: (b, i, g)),
        out_shape=jax.ShapeDtypeStruct((batch, seq, MIX_WIDTH), BF16),
        scratch_shapes=[pltpu.VMEM((MLA_HEADS_PER_STEP, ATTN_TQ, LANES), F32)] * 3,
        compiler_params=_attn_params(),
        name="mla_attention",
    )(q, k, v)
    return o.reshape(batch * seq, MIX_WIDTH)


def _sb_attn_kernel(q_ref, k_ref, v_ref, o_ref, later_scr, acc_scr):
    qi = pl.program_id(2)
    t = ATTN_TQ
    row = lax.broadcasted_iota(jnp.int32, (t, t), 0)
    col = lax.broadcasted_iota(jnp.int32, (t, t), 1)
    strict = col < row
    suffix = jnp.where(row > col, 1.0, 0.0).astype(BF16)
    first = _first_head_lanes((t, LANES))
    heads = range(SB_HEADS_PER_STEP)
    lanes = [slice(h // HEADS_PER_BLOCK * LANES, (h // HEADS_PER_BLOCK + 1) * LANES) for h in heads]
    qs = []
    for h in heads:
        q2 = q_ref[0, :, lanes[h]]
        mine = first if h % HEADS_PER_BLOCK == 0 else jnp.logical_not(first)
        qs.append(jnp.where(mine, q2, jnp.zeros_like(q2)))


    def gate_logs(z, visible=None):
        soft = jnp.log(1.0 + jnp.exp2(-jnp.abs(z))) * LOG2_E
        log_beta = jnp.minimum(z, 0.0) - soft
        log_rest = log_beta - z
        if visible is not None:
            log_rest = jnp.where(visible, log_rest, 0.0)
        return log_beta, log_rest

    def row_sums(x):
        return jnp.broadcast_to(jnp.sum(x, axis=-1, keepdims=True), (x.shape[0], LANES))

    def diagonal_block():
        half = t // 2
        parts = [(slice(0, half), half), (slice(half, t), t)]
        chains = [(h, rows, width) for h in heads for rows, width in parts]
        zs = [_dot_nt(qs[h][rows], k_ref[0, pl.ds(diag, width), lanes[h]]) for h, rows, width in chains]
        logs = [gate_logs(z, strict[rows, :width]) for z, (_, rows, width) in zip(zs, chains)]
        sums = [_dot(log_rest.astype(BF16), suffix[:width, :width])
                for (_, log_rest), (_, _, width) in zip(logs, chains)]
        weights = [jnp.where(strict[rows, :width], jnp.exp2(log_beta + s), 0.0).astype(BF16)
                   for (log_beta, _), s, (_, rows, width) in zip(logs, sums, chains)]
        highest = None
        for h in heads:
            (_, _, w0), (_, _, w1) = chains[2 * h], chains[2 * h + 1]
            later = jnp.concatenate([row_sums(logs[2 * h][1]), row_sums(logs[2 * h + 1][1])], axis=0)
            later_scr[h] = later
            acc_scr[h] = jnp.concatenate(
                [_dot(weights[2 * h], v_ref[0, pl.ds(diag, w0), lanes[h]]),
                 _dot(weights[2 * h + 1], v_ref[0, pl.ds(diag, w1), lanes[h]])], axis=0)
            top = jnp.max(later)
            highest = top if highest is None else jnp.maximum(highest, top)
        return highest

    def earlier_block(start):
        zs = [_dot_nt(qs[h], k_ref[0, pl.ds(start, t), lanes[h]]) for h in heads]
        logs = [gate_logs(z) for z in zs]
        sums = [_dot(log_rest.astype(BF16), suffix) for _, log_rest in logs]
        highest = None
        weights = []
        for h in heads:
            later = later_scr[h]
            weights.append(jnp.exp2(logs[h][0] + sums[h] + _widen(later, t)).astype(BF16))
            later = later + row_sums(logs[h][1])
            later_scr[h] = later
            top = jnp.max(later)
            highest = top if highest is None else jnp.maximum(highest, top)
        for h in heads:
            acc_scr[h] += _dot(weights[h], v_ref[0, pl.ds(start, t), lanes[h]])
        return highest

    diag = pl.multiple_of(qi * t, t)
    highest_later = diagonal_block()

    def more(c):
        blocks_done, highest = c
        return jnp.logical_and(blocks_done < qi, highest > SB_SKIP_BELOW)

    def one_block(c):
        blocks_done, _ = c
        return blocks_done + 1, earlier_block(pl.multiple_of(diag - (blocks_done + 1) * t, t))

    lax.while_loop(more, one_block, (jnp.int32(0), highest_later))

    for b in range(SB_HEADS_PER_STEP // HEADS_PER_BLOCK):
        o_ref[0, :, b * LANES:(b + 1) * LANES] = jnp.where(
            first, acc_scr[HEADS_PER_BLOCK * b], acc_scr[HEADS_PER_BLOCK * b + 1]).astype(BF16)


def _sb_attention(q, k, v, batch, seq):
    q = q.reshape(batch, seq, MIX_WIDTH)
    k = k.reshape(batch, seq, MIX_WIDTH)
    v = v.reshape(batch, seq, MIX_WIDTH)
    width = SB_HEADS_PER_STEP * HEAD_DIM
    whole = pl.BlockSpec((1, seq, width), lambda b, g, i: (b, 0, g))
    tile = pl.BlockSpec((1, ATTN_TQ, width), lambda b, g, i: (b, i, g))
    o = pl.pallas_call(
        _sb_attn_kernel,
        grid=(batch, N_HEADS // SB_HEADS_PER_STEP, seq // ATTN_TQ),
        in_specs=[tile, whole, whole],
        out_specs=tile,
        out_shape=jax.ShapeDtypeStruct((batch, seq, MIX_WIDTH), BF16),
        scratch_shapes=[pltpu.VMEM((SB_HEADS_PER_STEP, ATTN_TQ, LANES), F32)] * 2,
        compiler_params=_attn_params(),
        name="sb_attention",
    )(q, k, v)
    return o.reshape(batch * seq, MIX_WIDTH)


def _post_kernel(o_ref, gate_ref, x_ref, p_ref, wout_ref, wg_ref, wp_ref, *rest, with_kv):
    gate = gate_ref[...].astype(F32)
    u = o_ref[...].astype(F32) * (gate * _sigmoid(gate))
    y = x_ref[...] + _dot(u.astype(BF16), wout_ref[...])
    ple = _dot(p_ref[...].astype(BF16), wp_ref[...])
    x_new = y + _sigmoid(_dot(y.astype(BF16), wg_ref[...])) * ple
    if with_kv:
        kvg_ref, wkv_ref, xo_ref, k_ref, v_ref = rest
        kv = _dot(_rms(x_new, kvg_ref[...]).astype(BF16), wkv_ref[...])
        k_ref[...] = kv[:, :MIX_WIDTH].astype(BF16)
        v_ref[...] = kv[:, MIX_WIDTH:].astype(BF16)
    else:
        (xo_ref,) = rest
    xo_ref[...] = x_new


def _post(o, gate, x, p, layer, w_out, w_gate, w_proj, kv_ln_g=None, w_kv=None):
    tokens = x.shape[0]
    with_kv = w_kv is not None
    p_spec = pl.BlockSpec((None, TOKEN_TILE, PLE_DIM), lambda i: (layer, i, 0))
    in_specs = [_row_spec(MIX_WIDTH), _row_spec(MIX_WIDTH), _row_spec(D_MODEL), p_spec,
                _const_spec((MIX_WIDTH, D_MODEL)), _const_spec((D_MODEL, D_MODEL)),
                _const_spec((PLE_DIM, D_MODEL))]
    args = [o, gate, x, p, w_out.astype(BF16), w_gate.astype(BF16), w_proj.astype(BF16)]
    out_specs = [_row_spec(D_MODEL)]
    out_shape = [jax.ShapeDtypeStruct((tokens, D_MODEL), F32)]
    if with_kv:
        in_specs += [_const_spec((1, D_MODEL)), _const_spec((D_MODEL, 2 * MIX_WIDTH))]
        args += [kv_ln_g.reshape(1, D_MODEL), w_kv.astype(BF16)]
        out_specs += [_row_spec(MIX_WIDTH), _row_spec(MIX_WIDTH)]
        out_shape += [jax.ShapeDtypeStruct((tokens, MIX_WIDTH), BF16)] * 2
    return pl.pallas_call(
        functools.partial(_post_kernel, with_kv=with_kv),
        grid=(tokens // TOKEN_TILE,),
        in_specs=in_specs, out_specs=out_specs, out_shape=out_shape,
        compiler_params=_params(),
        name="layer_tail_kv" if with_kv else "layer_tail",
    )(*args)


def _sb_front_kernel(x_ref, ln_ref, wq_ref, wgate_ref, q_ref, gate_ref):
    hb = _rms(x_ref[...], ln_ref[...]).astype(BF16)
    q_ref[...] = (_dot(hb, wq_ref[...]) * (HEAD_DIM ** -0.5 * LOG2_E)).astype(BF16)
    gate_ref[...] = _dot(hb, wgate_ref[...]).astype(BF16)


def _sb_front(x, ln_g, w_in):
    tokens = x.shape[0]
    out = jax.ShapeDtypeStruct((tokens, MIX_WIDTH), BF16)
    return pl.pallas_call(
        _sb_front_kernel,
        grid=(tokens // TOKEN_TILE,),
        in_specs=[_row_spec(D_MODEL), _const_spec((1, D_MODEL)),
                  _const_spec((D_MODEL, MIX_WIDTH)), _const_spec((D_MODEL, MIX_WIDTH))],
        out_specs=[_row_spec(MIX_WIDTH), _row_spec(MIX_WIDTH)],
        out_shape=[out, out],
        compiler_params=_params(),
        name="sb_front",
    )(x, ln_g.reshape(1, D_MODEL), w_in[:, :MIX_WIDTH].astype(BF16), w_in[:, MIX_WIDTH:].astype(BF16))


def kernel(x, p, positions, mla_ln_g, mla_w_in, mla_q_norm_g, mla_kv_norm_g, mla_w_q_up, mla_w_kv_up, mla_q_head_g, mla_k_head_g, mla_w_out, kv_ln_g, w_kv_shared, sb_ln_g, sb_w_in, sb_w_out, ple_w_proj, ple_w_gate):
    batch, seq, _ = x.shape
    tokens = batch * seq
    x = x.reshape(tokens, D_MODEL)
    p = p.reshape(DEPTH, tokens, PLE_DIM)
    cos_t, sin_t = _rope_tables(positions)
    k_sh = v_sh = None
    for i in range(DEPTH):
        if i < N_A:
            q, k, v, gate = _mla_front(x, mla_ln_g[i], mla_w_in[i], mla_q_norm_g[i], mla_kv_norm_g[i],
                                       mla_w_q_up[i], mla_w_kv_up[i], mla_q_head_g[i], mla_k_head_g[i],
                                       cos_t, sin_t)
            o = _mla_attention(q, k, v, batch, seq)
            w_out = mla_w_out[i]
        else:
            j = i - N_A
            q, gate = _sb_front(x, sb_ln_g[j], sb_w_in[j])
            o = _sb_attention(q, k_sh, v_sh, batch, seq)
            w_out = sb_w_out[j]
        if i == N_A - 1:
            x, k_sh, v_sh = _post(o, gate, x, p, i, w_out, ple_w_gate[i], ple_w_proj[i], kv_ln_g, w_kv_shared)
        else:
            (x,) = _post(o, gate, x, p, i, w_out, ple_w_gate[i], ple_w_proj[i])
    return x.reshape(batch, seq, D_MODEL)
```

```python
import functools

import jax
import jax.numpy as jnp
from jax import lax
from jax.experimental import pallas as pl
from jax.experimental.pallas import tpu as pltpu

D_MODEL = 1024
DEPTH = 4
N_A = DEPTH // 2
PLE_DIM = 256
N_HEADS = 16
HEAD_DIM = 64
ROPE_DIM = 32
ROPE_HALF = ROPE_DIM // 2
QK_DIM = HEAD_DIM + ROPE_DIM
Q_LORA = 384
KV_LORA = 256
MIX_WIDTH = N_HEADS * HEAD_DIM
ROPE_THETA = 10000.0
EPS = 1e-6

LANES = 128
HEADS_PER_BLOCK = LANES // HEAD_DIM
N_PAIRS = N_HEADS // HEADS_PER_BLOCK
MLA_PAD = N_HEADS * LANES
TOKEN_TILE = 512
FRONT_ROWS = 256
ATTN_TQ = 256
KEY_BLOCK = 512
MLA_HEADS_PER_STEP = 8
SB_HEADS_PER_STEP = 8
SB_SKIP_BELOW = -200.0
LOG2_E = 1.4426950408889634
VMEM_LIMIT = 48 * 1024 * 1024

F32 = jnp.float32
BF16 = jnp.bfloat16


def _dot(a, b):
    return jnp.dot(a, b, preferred_element_type=F32)


def _dot_nt(a, b):
    return lax.dot_general(a, b, (((1,), (1,)), ((), ())), preferred_element_type=F32)


def _rms(x, g):
    return x * lax.rsqrt(jnp.mean(x * x, axis=-1, keepdims=True) + EPS) * g


def _sigmoid(x):
    return 1.0 / (1.0 + jnp.exp(-x))


def _params():
    return pltpu.CompilerParams(dimension_semantics=("arbitrary",), vmem_limit_bytes=VMEM_LIMIT)


def _const_spec(shape):
    return pl.BlockSpec(shape, lambda i: (0,) * len(shape))


def _row_spec(width, tile=TOKEN_TILE):
    return pl.BlockSpec((tile, width), lambda i: (i, 0))


def _rope_table_kernel(pos_ref, inv_ref, sign_ref, keep_ref, cos_ref, sin_ref):
    ang = pos_ref[...] * inv_ref[...]
    cos_ref[...] = jnp.cos(ang) * keep_ref[...]
    sin_ref[...] = jnp.sin(ang) * sign_ref[...]


def _head_block_row(nope, first, second):
    pad = jnp.zeros((LANES - QK_DIM - ROPE_HALF,), F32)
    return jnp.concatenate([nope, first, second, first, pad]).reshape(1, LANES)


def _rope_tables(positions):
    tokens = positions.size
    pos = positions.astype(F32).reshape(tokens, 1)
    inv = 1.0 / (ROPE_THETA ** (jnp.arange(ROPE_HALF, dtype=F32) / ROPE_HALF))
    zeros = jnp.zeros((HEAD_DIM,), F32)
    ones = jnp.ones((ROPE_HALF,), F32)
    inv_row = _head_block_row(zeros, inv, inv)
    sign_row = _head_block_row(zeros, -ones, ones) * _keep_row()
    out = jax.ShapeDtypeStruct((tokens, LANES), F32)
    return pl.pallas_call(
        _rope_table_kernel,
        grid=(tokens // TOKEN_TILE,),
        in_specs=[_row_spec(1), _const_spec((1, LANES)), _const_spec((1, LANES)), _const_spec((1, LANES))],
        out_specs=[_row_spec(LANES), _row_spec(LANES)],
        out_shape=[out, out],
        compiler_params=_params(),
        name="rope_tables",
    )(pos, inv_row, sign_row, _keep_row())


def _keep_row():
    return (jnp.arange(LANES) < QK_DIM).astype(F32).reshape(1, LANES)


def _mla_front_kernel(x_ref, ln_ref, wcq_ref, wckv_ref, wkr_ref, wgate_ref, qn_ref, kvn_ref,
                      wq_ref, wk_ref, wv_ref, gq_ref, gk_ref, keep_ref, cos_ref, sin_ref,
                      q_ref, k_ref, v_ref, gate_ref):
    keep = keep_ref[...]
    gq = gq_ref[...]
    gk = gk_ref[...]
    pair_cols = [slice(p * 2 * LANES, (p + 1) * 2 * LANES) for p in range(N_PAIRS)]
    head_blocks = [(p, slice(j * LANES, (j + 1) * LANES))
                   for p in range(N_PAIRS) for j in range(HEADS_PER_BLOCK)]
    ones_row = lax.broadcasted_iota(jnp.int32, (2 * LANES, 2 * LANES), 0)
    ones_col = lax.broadcasted_iota(jnp.int32, (2 * LANES, 2 * LANES), 1)
    same_block = ones_row // LANES == ones_col // LANES
    block_ones_k = jnp.where(same_block, 1.0, 0.0).astype(BF16)
    real_lane = jnp.bitwise_and(ones_row, LANES - 1) < QK_DIM
    block_ones_q = jnp.where(jnp.logical_and(same_block, real_lane), 1.0, 0.0).astype(BF16)

    def project(rows):
        hb = _rms(x_ref[rows, :], ln_ref[...]).astype(BF16)
        cq = _dot(hb, wcq_ref[...])
        ckv = _dot(hb, wckv_ref[...])
        kr = _dot(hb, wkr_ref[...])
        gate_ref[rows, :] = _dot(hb, wgate_ref[...]).astype(BF16)
        cqn = _rms(cq, qn_ref[...]).astype(BF16)
        ckvn = _rms(ckv, kvn_ref[...]).astype(BF16)
        v_ref[rows, :] = _dot(ckvn, wv_ref[...]).astype(BF16)
        kn2 = [_dot(ckvn, wk_ref[:, c]) for c in pair_cols]
        qh2 = [_dot(cqn, wq_ref[:, c]) for c in pair_cols]
        return kr, kn2, qh2

    def finish(rows, kr, kn2, qh2):
        cos = cos_ref[rows, :]
        sin = sin_ref[rows, :]

        def rope(y):
            return y * cos + pltpu.roll(y, LANES - ROPE_HALF, 1) * sin

        kns = [kn2[p][:, blk] for p, blk in head_blocks]
        qhs = [qh2[p][:, blk] for p, blk in head_blocks]
        ss_kr = jnp.sum(kr * kr * keep, axis=-1, keepdims=True) + QK_DIM * EPS
        ss_k2 = [_dot((kn * kn).astype(BF16), block_ones_k) for kn in kn2]
        ss_q2 = [_dot((qh * qh).astype(BF16), block_ones_q) for qh in qh2]
        r_k = [lax.rsqrt(ss_k2[p][:, blk] + ss_kr) for p, blk in head_blocks]
        r_q = [lax.rsqrt(ss_q2[p][:, blk] + QK_DIM * EPS) for p, blk in head_blocks]
        k_rope = rope(kr * gk)
        q_rot = [rope(qh * gq) for qh in qhs]
        for h in range(N_HEADS):
            out = slice(h * LANES, (h + 1) * LANES)
            k_ref[rows, out] = ((kns[h] * gk + k_rope) * r_k[h]).astype(BF16)
            q_ref[rows, out] = (q_rot[h] * r_q[h]).astype(BF16)

    groups = [slice(r, r + FRONT_ROWS) for r in range(0, TOKEN_TILE, FRONT_ROWS)]
    projected = [project(rows) for rows in groups]
    for rows, raw in zip(groups, projected):
        finish(rows, *raw)


def _rotary_block(w):
    first = w[..., HEAD_DIM:HEAD_DIM + ROPE_HALF]
    pad = jnp.zeros(w.shape[:-1] + (LANES - QK_DIM - ROPE_HALF,), w.dtype)
    return jnp.concatenate([w, first, pad], axis=-1)


def _pad_heads(w, width):
    k = w.shape[0]
    w = w.reshape(k, N_HEADS, width)
    w = jnp.pad(w, ((0, 0), (0, 0), (0, LANES - width)))
    return w.reshape(k, MLA_PAD)


def _mla_front(x, ln_g, w_in, qn_g, kvn_g, w_q_up, w_kv_up, q_head_g, k_head_g, cos_t, sin_t):
    tokens = x.shape[0]
    wcq = w_in[:, :Q_LORA].astype(BF16)
    wckv = w_in[:, Q_LORA:Q_LORA + KV_LORA].astype(BF16)
    wkr = w_in[:, Q_LORA + KV_LORA:Q_LORA + KV_LORA + ROPE_DIM]
    wkr = _rotary_block(jnp.pad(wkr, ((0, 0), (HEAD_DIM, 0)))).astype(BF16)
    wgate = w_in[:, Q_LORA + KV_LORA + ROPE_DIM:].astype(BF16)
    wq = _rotary_block(w_q_up.reshape(Q_LORA, N_HEADS, QK_DIM)).reshape(Q_LORA, MLA_PAD).astype(BF16)
    wkv = w_kv_up.reshape(KV_LORA, N_HEADS, 2 * HEAD_DIM)
    wk = _pad_heads(wkv[:, :, :HEAD_DIM].reshape(KV_LORA, MIX_WIDTH), HEAD_DIM).astype(BF16)
    wv = wkv[:, :, HEAD_DIM:].reshape(KV_LORA, MIX_WIDTH).astype(BF16)
    k_fold = QK_DIM ** 0.5
    q_fold = LOG2_E
    out = lambda w: jax.ShapeDtypeStruct((tokens, w), BF16)
    return pl.pallas_call(
        _mla_front_kernel,
        grid=(tokens // TOKEN_TILE,),
        in_specs=[_row_spec(D_MODEL), _const_spec((1, D_MODEL)),
                  _const_spec((D_MODEL, Q_LORA)), _const_spec((D_MODEL, KV_LORA)),
                  _const_spec((D_MODEL, LANES)), _const_spec((D_MODEL, MIX_WIDTH)),
                  _const_spec((1, Q_LORA)), _const_spec((1, KV_LORA)),
                  _const_spec((Q_LORA, MLA_PAD)), _const_spec((KV_LORA, MLA_PAD)),
                  _const_spec((KV_LORA, MIX_WIDTH)),
                  _const_spec((1, LANES)), _const_spec((1, LANES)), _const_spec((1, LANES)),
                  _row_spec(LANES), _row_spec(LANES)],
        out_specs=[_row_spec(MLA_PAD), _row_spec(MLA_PAD), _row_spec(MIX_WIDTH), _row_spec(MIX_WIDTH)],
        out_shape=[out(MLA_PAD), out(MLA_PAD), out(MIX_WIDTH), out(MIX_WIDTH)],
        compiler_params=_params(),
        name="mla_front",
    )(x, ln_g.reshape(1, D_MODEL), wcq, wckv, wkr, wgate, qn_g.reshape(1, Q_LORA),
      kvn_g.reshape(1, KV_LORA), wq, wk, wv, _rotary_block(q_head_g * q_fold).reshape(1, LANES),
      _rotary_block(k_head_g * k_fold).reshape(1, LANES), _keep_row(), cos_t, sin_t)


def _attn_params():
    return pltpu.CompilerParams(dimension_semantics=("arbitrary",) * 3, vmem_limit_bytes=VMEM_LIMIT)


def _first_head_lanes(shape):
    return lax.broadcasted_iota(jnp.int32, shape, 1) < HEAD_DIM


def _is_odd(i):
    return jnp.bitwise_and(i, 1) == 1


def _widen(x, width):
    return jnp.concatenate([x] * (width // LANES), axis=1)


def _mla_attn_kernel(q_ref, k_ref, v_ref, o_ref, m_scr, l_scr, acc_scr):
    qi = pl.program_id(2)
    n_wide = qi // (KEY_BLOCK // ATTN_TQ)
    row = lax.broadcasted_iota(jnp.int32, (ATTN_TQ, ATTN_TQ), 0)
    col = lax.broadcasted_iota(jnp.int32, (ATTN_TQ, ATTN_TQ), 1)
    causal = col <= row
    heads = range(MLA_HEADS_PER_STEP)
    qs = [q_ref[0, :, h * LANES:(h + 1) * LANES] for h in heads]

    def step(start, width, visible=None, fresh=False):
        scores = [_dot_nt(qs[h], k_ref[0, pl.ds(start, width), h * LANES:(h + 1) * LANES])
                  for h in heads]
        for h in heads:
            s = scores[h] if visible is None else jnp.where(visible, scores[h], -jnp.inf)
            block_max = jnp.broadcast_to(jnp.max(s, axis=-1, keepdims=True), (ATTN_TQ, LANES))
            m_new = block_max if fresh else jnp.maximum(m_scr[h], block_max)
            p = jnp.exp2(s - _widen(m_new, width))
            block_sum = jnp.broadcast_to(jnp.sum(p, axis=-1, keepdims=True), (ATTN_TQ, LANES))
            alpha = None if fresh else jnp.exp2(m_scr[h] - m_new)
            l_scr[h] = block_sum if fresh else alpha * l_scr[h] + block_sum
            m_scr[h] = m_new
            pair = h // HEADS_PER_BLOCK
            pv = _dot(p.astype(BF16), v_ref[0, pl.ds(start, width), pair * LANES:(pair + 1) * LANES])
            acc_scr[h] = pv if fresh else alpha * acc_scr[h] + pv

    step(pl.multiple_of(qi * ATTN_TQ, ATTN_TQ), ATTN_TQ, causal, fresh=True)

    @pl.loop(0, n_wide)
    def _(kb):
        step(pl.multiple_of(kb * KEY_BLOCK, KEY_BLOCK), KEY_BLOCK)

    @pl.when(_is_odd(qi))
    def _():
        step(pl.multiple_of(n_wide * KEY_BLOCK, KEY_BLOCK), ATTN_TQ)

    first = _first_head_lanes((ATTN_TQ, LANES))
    for b in range(MLA_HEADS_PER_STEP // HEADS_PER_BLOCK):
        o0 = acc_scr[2 * b] / l_scr[2 * b]
        o1 = acc_scr[2 * b + 1] / l_scr[2 * b + 1]
        o_ref[0, :, b * LANES:(b + 1) * LANES] = jnp.where(first, o0, o1).astype(BF16)


def _mla_attention(q, k, v, batch, seq):
    q = q.reshape(batch, seq, MLA_PAD)
    k = k.reshape(batch, seq, MLA_PAD)
    v = v.reshape(batch, seq, MIX_WIDTH)
    qk_w = MLA_HEADS_PER_STEP * LANES
    v_w = MLA_HEADS_PER_STEP * HEAD_DIM
    o = pl.pallas_call(
        _mla_attn_kernel,
        grid=(batch, N_HEADS // MLA_HEADS_PER_STEP, seq // ATTN_TQ),
        in_specs=[pl.BlockSpec((1, ATTN_TQ, qk_w), lambda b, g, i: (b, i, g)),
                  pl.BlockSpec((1, seq, qk_w), lambda b, g, i: (b, 0, g)),
                  pl.BlockSpec((1, seq, v_w), lambda b, g, i: (b, 0, g))],
        out_specs=pl.BlockSpec((1, ATTN_TQ, v_w), lambda b, g, i: (b, i, g)),
        out_shape=jax.ShapeDtypeStruct((batch, seq, MIX_WIDTH), BF16),
        scratch_shapes=[pltpu.VMEM((MLA_HEADS_PER_STEP, ATTN_TQ, LANES), F32)] * 3,
        compiler_params=_attn_params(),
        name="mla_attention",
    )(q, k, v)
    return o.reshape(batch * seq, MIX_WIDTH)


def _sb_attn_kernel(q_ref, k_ref, v_ref, o_ref, later_scr, acc_scr):
    qi = pl.program_id(2)
    t = ATTN_TQ
    row = lax.broadcasted_iota(jnp.int32, (t, t), 0)
    col = lax.broadcasted_iota(jnp.int32, (t, t), 1)
    strict = col < row
    suffix = jnp.where(row > col, 1.0, 0.0).astype(BF16)
    first = _first_head_lanes((t, LANES))
    heads = range(SB_HEADS_PER_STEP)
    lanes = [slice(h // HEADS_PER_BLOCK * LANES, (h // HEADS_PER_BLOCK + 1) * LANES) for h in heads]
    qs = []
    for h in heads:
        q2 = q_ref[0, :, lanes[h]]
        mine = first if h % HEADS_PER_BLOCK == 0 else jnp.logical_not(first)
        qs.append(jnp.where(mine, q2, jnp.zeros_like(q2)))


    def gate_logs(z, visible=None):
        soft = jnp.log(1.0 + jnp.exp2(-jnp.abs(z))) * LOG2_E
        log_beta = jnp.minimum(z, 0.0) - soft
        log_rest = log_beta - z
        if visible is not None:
            log_rest = jnp.where(visible, log_rest, 0.0)
        return log_beta, log_rest

    def row_sums(x):
        return jnp.broadcast_to(jnp.sum(x, axis=-1, keepdims=True), (x.shape[0], LANES))

    def diagonal_block():
        half = t // 2
        parts = [(slice(0, half), half), (slice(half, t), t)]
        chains = [(h, rows, width) for h in heads for rows, width in parts]
        zs = [_dot_nt(qs[h][rows], k_ref[0, pl.ds(diag, width), lanes[h]]) for h, rows, width in chains]
        logs = [gate_logs(z, strict[rows, :width]) for z, (_, rows, width) in zip(zs, chains)]
        sums = [_dot(log_rest.astype(BF16), suffix[:width, :width])
                for (_, log_rest), (_, _, width) in zip(logs, chains)]
        weights = [jnp.where(strict[rows, :width], jnp.exp2(log_beta + s), 0.0).astype(BF16)
                   for (log_beta, _), s, (_, rows, width) in zip(logs, sums, chains)]
        highest = None
        for h in heads:
            (_, _, w0), (_, _, w1) = chains[2 * h], chains[2 * h + 1]
            later = jnp.concatenate([row_sums(logs[2 * h][1]), row_sums(logs[2 * h + 1][1])], axis=0)
            later_scr[h] = later
            acc_scr[h] = jnp.concatenate(
                [_dot(weights[2 * h], v_ref[0, pl.ds(diag, w0), lanes[h]]),
                 _dot(weights[2 * h + 1], v_ref[0, pl.ds(diag, w1), lanes[h]])], axis=0)
            top = jnp.max(later)
            highest = top if highest is None else jnp.maximum(highest, top)
        return highest

    def earlier_block(start):
        zs = [_dot_nt(qs[h], k_ref[0, pl.ds(start, t), lanes[h]]) for h in heads]
        logs = [gate_logs(z) for z in zs]
        sums = [_dot(log_rest.astype(BF16), suffix) for _, log_rest in logs]
        highest = None
        weights = []
        for h in heads:
            later = later_scr[h]
            weights.append(jnp.exp2(logs[h][0] + sums[h] + _widen(later, t)).astype(BF16))
            later = later + row_sums(logs[h][1])
            later_scr[h] = later
            top = jnp.max(later)
            highest = top if highest is None else jnp.maximum(highest, top)
        for h in heads:
            acc_scr[h] += _dot(weights[h], v_ref[0, pl.ds(start, t), lanes[h]])
        return highest

    diag = pl.multiple_of(qi * t, t)
    highest_later = diagonal_block()

    def more(c):
        blocks_done, highest = c
        return jnp.logical_and(blocks_done < qi, highest > SB_SKIP_BELOW)

    def one_block(c):
        blocks_done, _ = c
        return blocks_done + 1, earlier_block(pl.multiple_of(diag - (blocks_done + 1) * t, t))

    lax.while_loop(more, one_block, (jnp.int32(0), highest_later))

    for b in range(SB_HEADS_PER_STEP // HEADS_PER_BLOCK):
        o_ref[0, :, b * LANES:(b + 1) * LANES] = jnp.where(
            first, acc_scr[HEADS_PER_BLOCK * b], acc_scr[HEADS_PER_BLOCK * b + 1]).astype(BF16)


def _sb_attention(q, k, v, batch, seq):
    q = q.reshape(batch, seq, MIX_WIDTH)
    k = k.reshape(batch, seq, MIX_WIDTH)
    v = v.reshape(batch, seq, MIX_WIDTH)
    width = SB_HEADS_PER_STEP * HEAD_DIM
    whole = pl.BlockSpec((1, seq, width), lambda b, g, i: (b, 0, g))
    tile = pl.BlockSpec((1, ATTN_TQ, width), lambda b, g, i: (b, i, g))
    o = pl.pallas_call(
        _sb_attn_kernel,
        grid=(batch, N_HEADS // SB_HEADS_PER_STEP, seq // ATTN_TQ),
        in_specs=[tile, whole, whole],
        out_specs=tile,
        out_shape=jax.ShapeDtypeStruct((batch, seq, MIX_WIDTH), BF16),
        scratch_shapes=[pltpu.VMEM((SB_HEADS_PER_STEP, ATTN_TQ, LANES), F32)] * 2,
        compiler_params=_attn_params(),
        name="sb_attention",
    )(q, k, v)
    return o.reshape(batch * seq, MIX_WIDTH)


def _post_kernel(o_ref, gate_ref, x_ref, p_ref, wout_ref, wg_ref, wp_ref, *rest, with_kv):
    gate = gate_ref[...].astype(F32)
    u = o_ref[...].astype(F32) * (gate * _sigmoid(gate))
    y = x_ref[...] + _dot(u.astype(BF16), wout_ref[...])
    ple = _dot(p_ref[...].astype(BF16), wp_ref[...])
    x_new = y + _sigmoid(_dot(y.astype(BF16), wg_ref[...])) * ple
    if with_kv:
        kvg_ref, wkv_ref, xo_ref, k_ref, v_ref = rest
        kv = _dot(_rms(x_new, kvg_ref[...]).astype(BF16), wkv_ref[...])
        k_ref[...] = kv[:, :MIX_WIDTH].astype(BF16)
        v_ref[...] = kv[:, MIX_WIDTH:].astype(BF16)
    else:
        (xo_ref,) = rest
    xo_ref[...] = x_new


def _layer_spec(rows, cols, layer, col_block=0):
    return pl.BlockSpec((None, rows, cols), lambda i: (layer, 0, col_block))


def _post(o, gate, x, p, layer, w_out, out_layer, w_gate, w_proj, kv_ln_g=None, w_kv=None):
    tokens = x.shape[0]
    with_kv = w_kv is not None
    p_spec = pl.BlockSpec((None, TOKEN_TILE, PLE_DIM), lambda i: (layer, i, 0))
    in_specs = [_row_spec(MIX_WIDTH), _row_spec(MIX_WIDTH), _row_spec(D_MODEL), p_spec,
                _layer_spec(MIX_WIDTH, D_MODEL, out_layer), _layer_spec(D_MODEL, D_MODEL, layer),
                _layer_spec(PLE_DIM, D_MODEL, layer)]
    args = [o, gate, x, p, w_out, w_gate, w_proj]
    out_specs = [_row_spec(D_MODEL)]
    out_shape = [jax.ShapeDtypeStruct((tokens, D_MODEL), F32)]
    if with_kv:
        in_specs += [_const_spec((1, D_MODEL)), _const_spec((D_MODEL, 2 * MIX_WIDTH))]
        args += [kv_ln_g.reshape(1, D_MODEL), w_kv.astype(BF16)]
        out_specs += [_row_spec(MIX_WIDTH), _row_spec(MIX_WIDTH)]
        out_shape += [jax.ShapeDtypeStruct((tokens, MIX_WIDTH), BF16)] * 2
    return pl.pallas_call(
        functools.partial(_post_kernel, with_kv=with_kv),
        grid=(tokens // TOKEN_TILE,),
        in_specs=in_specs, out_specs=out_specs, out_shape=out_shape,
        compiler_params=_params(),
        name="layer_tail_kv" if with_kv else "layer_tail",
    )(*args)


def _sb_front_kernel(x_ref, ln_ref, wq_ref, wgate_ref, q_ref, gate_ref):
    hb = _rms(x_ref[...], ln_ref[...]).astype(BF16)
    q_ref[...] = (_dot(hb, wq_ref[...]) * (HEAD_DIM ** -0.5 * LOG2_E)).astype(BF16)
    gate_ref[...] = _dot(hb, wgate_ref[...]).astype(BF16)


def _sb_front(x, ln_g, w_in, layer):
    tokens = x.shape[0]
    out = jax.ShapeDtypeStruct((tokens, MIX_WIDTH), BF16)
    return pl.pallas_call(
        _sb_front_kernel,
        grid=(tokens // TOKEN_TILE,),
        in_specs=[_row_spec(D_MODEL), _const_spec((1, D_MODEL)),
                  _layer_spec(D_MODEL, MIX_WIDTH, layer, 0), _layer_spec(D_MODEL, MIX_WIDTH, layer, 1)],
        out_specs=[_row_spec(MIX_WIDTH), _row_spec(MIX_WIDTH)],
        out_shape=[out, out],
        compiler_params=_params(),
        name="sb_front",
    )(x, ln_g.reshape(1, D_MODEL), w_in, w_in)


def kernel(x, p, positions, mla_ln_g, mla_w_in, mla_q_norm_g, mla_kv_norm_g, mla_w_q_up, mla_w_kv_up, mla_q_head_g, mla_k_head_g, mla_w_out, kv_ln_g, w_kv_shared, sb_ln_g, sb_w_in, sb_w_out, ple_w_proj, ple_w_gate):
    batch, seq, _ = x.shape
    tokens = batch * seq
    x = x.reshape(tokens, D_MODEL)
    p = p.reshape(DEPTH, tokens, PLE_DIM)
    cos_t, sin_t = _rope_tables(positions)
    mla_w_out, sb_w_out, sb_w_in = mla_w_out.astype(BF16), sb_w_out.astype(BF16), sb_w_in.astype(BF16)
    ple_w_gate, ple_w_proj = ple_w_gate.astype(BF16), ple_w_proj.astype(BF16)
    k_sh = v_sh = None
    for i in range(DEPTH):
        if i < N_A:
            q, k, v, gate = _mla_front(x, mla_ln_g[i], mla_w_in[i], mla_q_norm_g[i], mla_kv_norm_g[i],
                                       mla_w_q_up[i], mla_w_kv_up[i], mla_q_head_g[i], mla_k_head_g[i],
                                       cos_t, sin_t)
            o = _mla_attention(q, k, v, batch, seq)
            w_out, out_layer = mla_w_out, i
        else:
            q, gate = _sb_front(x, sb_ln_g[i - N_A], sb_w_in, i - N_A)
            o = _sb_attention(q, k_sh, v_sh, batch, seq)
            w_out, out_layer = sb_w_out, i - N_A
        if i == N_A - 1:
            x, k_sh, v_sh = _post(o, gate, x, p, i, w_out, out_layer, ple_w_gate, ple_w_proj, kv_ln_g, w_kv_shared)
        else:
            (x,) = _post(o, gate, x, p, i, w_out, out_layer, ple_w_gate, ple_w_proj)
    return x.reshape(batch, seq, D_MODEL)
```

```python
import functools

import jax
import jax.numpy as jnp
from jax import lax
from jax.experimental import pallas as pl
from jax.experimental.pallas import tpu as pltpu

D_MODEL = 1024
DEPTH = 4
N_A = DEPTH // 2
PLE_DIM = 256
N_HEADS = 16
HEAD_DIM = 64
ROPE_DIM = 32
ROPE_HALF = ROPE_DIM // 2
QK_DIM = HEAD_DIM + ROPE_DIM
Q_LORA = 384
KV_LORA = 256
MIX_WIDTH = N_HEADS * HEAD_DIM
ROPE_THETA = 10000.0
EPS = 1e-6

LANES = 128
HEADS_PER_BLOCK = LANES // HEAD_DIM
N_PAIRS = N_HEADS // HEADS_PER_BLOCK
MLA_PAD = N_HEADS * LANES
TOKEN_TILE = 512
FRONT_ROWS = 256
ATTN_TQ = 256
KEY_BLOCK = 512
MLA_HEADS_PER_STEP = 8
SB_HEADS_PER_STEP = 8
SB_SKIP_BELOW = -200.0
LOG2_E = 1.4426950408889634
VMEM_LIMIT = 48 * 1024 * 1024

F32 = jnp.float32
BF16 = jnp.bfloat16


def _dot(a, b):
    return jnp.dot(a, b, preferred_element_type=F32)


def _dot_nt(a, b):
    return lax.dot_general(a, b, (((1,), (1,)), ((), ())), preferred_element_type=F32)


def _rms(x, g):
    return x * lax.rsqrt(jnp.mean(x * x, axis=-1, keepdims=True) + EPS) * g


def _sigmoid(x):
    return 1.0 / (1.0 + jnp.exp(-x))


def _params():
    return pltpu.CompilerParams(dimension_semantics=("arbitrary",), vmem_limit_bytes=VMEM_LIMIT)


def _const_spec(shape):
    return pl.BlockSpec(shape, lambda i: (0,) * len(shape))


def _row_spec(width, tile=TOKEN_TILE):
    return pl.BlockSpec((tile, width), lambda i: (i, 0))


def _rope_table_kernel(pos_ref, inv_ref, sign_ref, keep_ref, cos_ref, sin_ref):
    ang = pos_ref[...] * inv_ref[...]
    cos_ref[...] = jnp.cos(ang) * keep_ref[...]
    sin_ref[...] = jnp.sin(ang) * sign_ref[...]


def _head_block_row(nope, first, second):
    pad = jnp.zeros((LANES - QK_DIM - ROPE_HALF,), F32)
    return jnp.concatenate([nope, first, second, first, pad]).reshape(1, LANES)


def _rope_tables(positions):
    tokens = positions.size
    pos = positions.astype(F32).reshape(tokens, 1)
    inv = 1.0 / (ROPE_THETA ** (jnp.arange(ROPE_HALF, dtype=F32) / ROPE_HALF))
    zeros = jnp.zeros((HEAD_DIM,), F32)
    ones = jnp.ones((ROPE_HALF,), F32)
    inv_row = _head_block_row(zeros, inv, inv)
    sign_row = _head_block_row(zeros, -ones, ones) * _keep_row()
    out = jax.ShapeDtypeStruct((tokens, LANES), F32)
    return pl.pallas_call(
        _rope_table_kernel,
        grid=(tokens // TOKEN_TILE,),
        in_specs=[_row_spec(1), _const_spec((1, LANES)), _const_spec((1, LANES)), _const_spec((1, LANES))],
        out_specs=[_row_spec(LANES), _row_spec(LANES)],
        out_shape=[out, out],
        compiler_params=_params(),
        name="rope_tables",
    )(pos, inv_row, sign_row, _keep_row())


def _keep_row():
    return (jnp.arange(LANES) < QK_DIM).astype(F32).reshape(1, LANES)


def _mla_front_kernel(x_ref, ln_ref, wcq_ref, wckv_ref, wkr_ref, wgate_ref, qn_ref, kvn_ref,
                      wq_ref, wk_ref, wv_ref, vone_ref, gq_ref, gk_ref, keep_ref, cos_ref, sin_ref,
                      q_ref, k_ref, v_ref, gate_ref):
    keep = keep_ref[...]
    gq = gq_ref[...]
    gk = gk_ref[...]
    pair_cols = [slice(p * 2 * LANES, (p + 1) * 2 * LANES) for p in range(N_PAIRS)]
    head_blocks = [(p, slice(j * LANES, (j + 1) * LANES))
                   for p in range(N_PAIRS) for j in range(HEADS_PER_BLOCK)]
    ones_row = lax.broadcasted_iota(jnp.int32, (2 * LANES, 2 * LANES), 0)
    ones_col = lax.broadcasted_iota(jnp.int32, (2 * LANES, 2 * LANES), 1)
    same_block = ones_row // LANES == ones_col // LANES
    block_ones_k = jnp.where(same_block, 1.0, 0.0).astype(BF16)
    real_lane = jnp.bitwise_and(ones_row, LANES - 1) < QK_DIM
    block_ones_q = jnp.where(jnp.logical_and(same_block, real_lane), 1.0, 0.0).astype(BF16)

    def project(rows):
        hb = _rms(x_ref[rows, :], ln_ref[...]).astype(BF16)
        cq = _dot(hb, wcq_ref[...])
        ckv = _dot(hb, wckv_ref[...])
        kr = _dot(hb, wkr_ref[...])
        gate_ref[rows, :] = _dot(hb, wgate_ref[...]).astype(BF16)
        cqn = _rms(cq, qn_ref[...]).astype(BF16)
        ckvn = _rms(ckv, kvn_ref[...]).astype(BF16)
        v_ref[rows, :] = (_dot(ckvn, wv_ref[...]) + vone_ref[...]).astype(BF16)
        kn2 = [_dot(ckvn, wk_ref[:, c]) for c in pair_cols]
        qh2 = [_dot(cqn, wq_ref[:, c]) for c in pair_cols]
        return kr, kn2, qh2

    def finish(rows, kr, kn2, qh2):
        cos = cos_ref[rows, :]
        sin = sin_ref[rows, :]

        def rope(y):
            return y * cos + pltpu.roll(y, LANES - ROPE_HALF, 1) * sin

        kns = [kn2[p][:, blk] for p, blk in head_blocks]
        qhs = [qh2[p][:, blk] for p, blk in head_blocks]
        ss_kr = jnp.sum(kr * kr * keep, axis=-1, keepdims=True) + QK_DIM * EPS
        ss_k2 = [_dot((kn * kn).astype(BF16), block_ones_k) for kn in kn2]
        ss_q2 = [_dot((qh * qh).astype(BF16), block_ones_q) for qh in qh2]
        r_k = [lax.rsqrt(ss_k2[p][:, blk] + ss_kr) for p, blk in head_blocks]
        r_q = [lax.rsqrt(ss_q2[p][:, blk] + QK_DIM * EPS) for p, blk in head_blocks]
        k_rope = rope(kr * gk)
        q_rot = [rope(qh * gq) for qh in qhs]
        for h in range(N_HEADS):
            out = slice(h * LANES, (h + 1) * LANES)
            k_ref[rows, out] = ((kns[h] * gk + k_rope) * r_k[h]).astype(BF16)
            q_ref[rows, out] = (q_rot[h] * r_q[h]).astype(BF16)

    groups = [slice(r, r + FRONT_ROWS) for r in range(0, TOKEN_TILE, FRONT_ROWS)]
    projected = [project(rows) for rows in groups]
    for rows, raw in zip(groups, projected):
        finish(rows, *raw)


def _rotary_block(w):
    first = w[..., HEAD_DIM:HEAD_DIM + ROPE_HALF]
    pad = jnp.zeros(w.shape[:-1] + (LANES - QK_DIM - ROPE_HALF,), w.dtype)
    return jnp.concatenate([w, first, pad], axis=-1)


def _pad_heads(w, width):
    k = w.shape[0]
    w = w.reshape(k, N_HEADS, width)
    w = jnp.pad(w, ((0, 0), (0, 0), (0, LANES - width)))
    return w.reshape(k, MLA_PAD)


def _mla_front(x, ln_g, w_in, qn_g, kvn_g, w_q_up, w_kv_up, q_head_g, k_head_g, cos_t, sin_t):
    tokens = x.shape[0]
    wcq = w_in[:, :Q_LORA].astype(BF16)
    wckv = w_in[:, Q_LORA:Q_LORA + KV_LORA].astype(BF16)
    wkr = w_in[:, Q_LORA + KV_LORA:Q_LORA + KV_LORA + ROPE_DIM]
    wkr = _rotary_block(jnp.pad(wkr, ((0, 0), (HEAD_DIM, 0)))).astype(BF16)
    wgate = w_in[:, Q_LORA + KV_LORA + ROPE_DIM:].astype(BF16)
    wq = _rotary_block(w_q_up.reshape(Q_LORA, N_HEADS, QK_DIM)).reshape(Q_LORA, MLA_PAD).astype(BF16)
    wkv = w_kv_up.reshape(KV_LORA, N_HEADS, 2 * HEAD_DIM)
    wk = _pad_heads(wkv[:, :, :HEAD_DIM].reshape(KV_LORA, MIX_WIDTH), HEAD_DIM).astype(BF16)
    wvh = wkv[:, :, HEAD_DIM:]
    odd_head = (jnp.arange(N_HEADS) % HEADS_PER_BLOCK == 1)[None, :, None]
    wv = jnp.where(odd_head, jnp.concatenate([jnp.zeros_like(wvh), wvh], axis=-1),
                   jnp.concatenate([wvh, jnp.zeros_like(wvh)], axis=-1)).reshape(KV_LORA, MLA_PAD).astype(BF16)
    value_lane = (jnp.arange(LANES) < HEAD_DIM)[None, :] != odd_head[0]
    v_one = jnp.where(value_lane, 0.0, 1.0).astype(F32).reshape(1, MLA_PAD)
    k_fold = QK_DIM ** 0.5
    q_fold = LOG2_E
    out = lambda w: jax.ShapeDtypeStruct((tokens, w), BF16)
    return pl.pallas_call(
        _mla_front_kernel,
        grid=(tokens // TOKEN_TILE,),
        in_specs=[_row_spec(D_MODEL), _const_spec((1, D_MODEL)),
                  _const_spec((D_MODEL, Q_LORA)), _const_spec((D_MODEL, KV_LORA)),
                  _const_spec((D_MODEL, LANES)), _const_spec((D_MODEL, MIX_WIDTH)),
                  _const_spec((1, Q_LORA)), _const_spec((1, KV_LORA)),
                  _const_spec((Q_LORA, MLA_PAD)), _const_spec((KV_LORA, MLA_PAD)),
                  _const_spec((KV_LORA, MLA_PAD)), _const_spec((1, MLA_PAD)),
                  _const_spec((1, LANES)), _const_spec((1, LANES)), _const_spec((1, LANES)),
                  _row_spec(LANES), _row_spec(LANES)],
        out_specs=[_row_spec(MLA_PAD), _row_spec(MLA_PAD), _row_spec(MLA_PAD), _row_spec(MIX_WIDTH)],
        out_shape=[out(MLA_PAD), out(MLA_PAD), out(MLA_PAD), out(MIX_WIDTH)],
        compiler_params=_params(),
        name="mla_front",
    )(x, ln_g.reshape(1, D_MODEL), wcq, wckv, wkr, wgate, qn_g.reshape(1, Q_LORA),
      kvn_g.reshape(1, KV_LORA), wq, wk, wv, v_one, _rotary_block(q_head_g * q_fold).reshape(1, LANES),
      _rotary_block(k_head_g * k_fold).reshape(1, LANES), _keep_row(), cos_t, sin_t)


def _attn_params():
    return pltpu.CompilerParams(dimension_semantics=("arbitrary",) * 3, vmem_limit_bytes=VMEM_LIMIT)


def _first_head_lanes(shape):
    return lax.broadcasted_iota(jnp.int32, shape, 1) < HEAD_DIM


def _is_odd(i):
    return jnp.bitwise_and(i, 1) == 1


def _widen(x, width):
    return jnp.concatenate([x] * (width // LANES), axis=1)


def _mla_attn_kernel(q_ref, k_ref, v_ref, o_ref, m_scr, acc_scr):
    qi = pl.program_id(2)
    n_wide = qi // (KEY_BLOCK // ATTN_TQ)
    row = lax.broadcasted_iota(jnp.int32, (ATTN_TQ, ATTN_TQ), 0)
    col = lax.broadcasted_iota(jnp.int32, (ATTN_TQ, ATTN_TQ), 1)
    causal = col <= row
    heads = range(MLA_HEADS_PER_STEP)
    qs = [q_ref[0, :, h * LANES:(h + 1) * LANES] for h in heads]

    def step(start, width, visible=None, fresh=False):
        scores = [_dot_nt(qs[h], k_ref[0, pl.ds(start, width), h * LANES:(h + 1) * LANES])
                  for h in heads]
        for h in heads:
            s = scores[h] if visible is None else jnp.where(visible, scores[h], -jnp.inf)
            block_max = jnp.broadcast_to(jnp.max(s, axis=-1, keepdims=True), (ATTN_TQ, LANES))
            m_new = block_max if fresh else jnp.maximum(m_scr[h], block_max)
            p = jnp.exp2(s - _widen(m_new, width))
            alpha = None if fresh else jnp.exp2(m_scr[h] - m_new)
            m_scr[h] = m_new
            pv = _dot(p.astype(BF16), v_ref[0, pl.ds(start, width), h * LANES:(h + 1) * LANES])
            acc_scr[h] = pv if fresh else alpha * acc_scr[h] + pv

    @pl.when(jnp.logical_not(_is_odd(qi)))
    def _():
        step(pl.multiple_of(qi * ATTN_TQ, ATTN_TQ), ATTN_TQ, causal, fresh=True)

    @pl.when(_is_odd(qi))
    def _():
        wide_row = lax.broadcasted_iota(jnp.int32, (ATTN_TQ, KEY_BLOCK), 0)
        wide_col = lax.broadcasted_iota(jnp.int32, (ATTN_TQ, KEY_BLOCK), 1)
        step(pl.multiple_of(n_wide * KEY_BLOCK, KEY_BLOCK), KEY_BLOCK,
             wide_col <= wide_row + (KEY_BLOCK - ATTN_TQ), fresh=True)

    @pl.loop(0, n_wide)
    def _(kb):
        step(pl.multiple_of(kb * KEY_BLOCK, KEY_BLOCK), KEY_BLOCK)

    first = _first_head_lanes((ATTN_TQ, LANES))
    for b in range(MLA_HEADS_PER_STEP // HEADS_PER_BLOCK):
        acc0, acc1 = acc_scr[HEADS_PER_BLOCK * b], acc_scr[HEADS_PER_BLOCK * b + 1]
        numerator = jnp.where(first, acc0, acc1)
        denominator = pltpu.roll(jnp.where(first, acc1, acc0), HEAD_DIM, 1)
        o_ref[0, :, b * LANES:(b + 1) * LANES] = (numerator / denominator).astype(BF16)


def _mla_attention(q, k, v, batch, seq):
    q = q.reshape(batch, seq, MLA_PAD)
    k = k.reshape(batch, seq, MLA_PAD)
    v = v.reshape(batch, seq, MLA_PAD)
    qk_w = MLA_HEADS_PER_STEP * LANES
    v_w = MLA_HEADS_PER_STEP * HEAD_DIM
    o = pl.pallas_call(
        _mla_attn_kernel,
        grid=(batch, N_HEADS // MLA_HEADS_PER_STEP, seq // ATTN_TQ),
        in_specs=[pl.BlockSpec((1, ATTN_TQ, qk_w), lambda b, g, i: (b, i, g)),
                  pl.BlockSpec((1, seq, qk_w), lambda b, g, i: (b, 0, g)),
                  pl.BlockSpec((1, seq, qk_w), lambda b, g, i: (b, 0, g))],
        out_specs=pl.BlockSpec((1, ATTN_TQ, v_w), lambda b, g, i: (b, i, g)),
        out_shape=jax.ShapeDtypeStruct((batch, seq, MIX_WIDTH), BF16),
        scratch_shapes=[pltpu.VMEM((MLA_HEADS_PER_STEP, ATTN_TQ, LANES), F32)] * 2,
        compiler_params=_attn_params(),
        name="mla_attention",
    )(q, k, v)
    return o.reshape(batch * seq, MIX_WIDTH)


def _sb_attn_kernel(q_ref, k_ref, v_ref, o_ref, later_scr, acc_scr):
    qi = pl.program_id(2)
    t = ATTN_TQ
    row = lax.broadcasted_iota(jnp.int32, (t, t), 0)
    col = lax.broadcasted_iota(jnp.int32, (t, t), 1)
    strict = col < row
    suffix = jnp.where(row > col, 1.0, 0.0).astype(BF16)
    suffix_and_total = jnp.concatenate([suffix, jnp.ones((t, LANES), BF16)], axis=1)
    first = _first_head_lanes((t, LANES))
    heads = range(SB_HEADS_PER_STEP)
    lanes = [slice(h // HEADS_PER_BLOCK * LANES, (h // HEADS_PER_BLOCK + 1) * LANES) for h in heads]
    qs = []
    for h in heads:
        q2 = q_ref[0, :, lanes[h]]
        mine = first if h % HEADS_PER_BLOCK == 0 else jnp.logical_not(first)
        qs.append(jnp.where(mine, q2, jnp.zeros_like(q2)))


    def gate_logs(z, visible=None):
        soft = jnp.log(1.0 + jnp.exp2(-jnp.abs(z))) * LOG2_E
        log_beta = jnp.minimum(z, 0.0) - soft
        log_rest = log_beta - z
        if visible is not None:
            log_rest = jnp.where(visible, log_rest, 0.0)
        return log_beta, log_rest

    def row_sums(x):
        return jnp.broadcast_to(jnp.sum(x, axis=-1, keepdims=True), (x.shape[0], LANES))

    def diagonal_block():
        half = t // 2
        parts = [(slice(0, half), half), (slice(half, t), t)]
        chains = [(h, rows, width) for h in heads for rows, width in parts]
        zs = [_dot_nt(qs[h][rows], k_ref[0, pl.ds(diag, width), lanes[h]]) for h, rows, width in chains]
        logs = [gate_logs(z, strict[rows, :width]) for z, (_, rows, width) in zip(zs, chains)]
        sums = [_dot(log_rest.astype(BF16), suffix[:width, :width])
                for (_, log_rest), (_, _, width) in zip(logs, chains)]
        weights = [jnp.where(strict[rows, :width], jnp.exp2(log_beta + s), 0.0).astype(BF16)
                   for (log_beta, _), s, (_, rows, width) in zip(logs, sums, chains)]
        highest = None
        for h in heads:
            (_, _, w0), (_, _, w1) = chains[2 * h], chains[2 * h + 1]
            later = jnp.concatenate([row_sums(logs[2 * h][1]), row_sums(logs[2 * h + 1][1])], axis=0)
            later_scr[h] = later
            acc_scr[h] = jnp.concatenate(
                [_dot(weights[2 * h], v_ref[0, pl.ds(diag, w0), lanes[h]]),
                 _dot(weights[2 * h + 1], v_ref[0, pl.ds(diag, w1), lanes[h]])], axis=0)
            top = jnp.max(later)
            highest = top if highest is None else jnp.maximum(highest, top)
        return highest

    def earlier_block(start):
        zs = [_dot_nt(qs[h], k_ref[0, pl.ds(start, t), lanes[h]]) for h in heads]
        logs = [gate_logs(z) for z in zs]
        sums = [_dot(log_rest.astype(BF16), suffix_and_total) for _, log_rest in logs]
        highest = None
        weights = []
        for h in heads:
            later = later_scr[h]
            weights.append(jnp.exp2(logs[h][0] + sums[h][:, :t] + _widen(later, t)).astype(BF16))
            later = later + sums[h][:, t:]
            later_scr[h] = later
            top = jnp.max(later)
            highest = top if highest is None else jnp.maximum(highest, top)
        for h in heads:
            acc_scr[h] += _dot(weights[h], v_ref[0, pl.ds(start, t), lanes[h]])
        return highest

    diag = pl.multiple_of(qi * t, t)
    highest_later = diagonal_block()

    def more(c):
        blocks_done, highest = c
        return jnp.logical_and(blocks_done < qi, highest > SB_SKIP_BELOW)

    def one_block(c):
        blocks_done, _ = c
        return blocks_done + 1, earlier_block(pl.multiple_of(diag - (blocks_done + 1) * t, t))

    lax.while_loop(more, one_block, (jnp.int32(0), highest_later))

    for b in range(SB_HEADS_PER_STEP // HEADS_PER_BLOCK):
        o_ref[0, :, b * LANES:(b + 1) * LANES] = jnp.where(
            first, acc_scr[HEADS_PER_BLOCK * b], acc_scr[HEADS_PER_BLOCK * b + 1]).astype(BF16)


def _sb_attention(q, k, v, batch, seq):
    q = q.reshape(batch, seq, MIX_WIDTH)
    k = k.reshape(batch, seq, MIX_WIDTH)
    v = v.reshape(batch, seq, MIX_WIDTH)
    width = SB_HEADS_PER_STEP * HEAD_DIM
    whole = pl.BlockSpec((1, seq, width), lambda b, g, i: (b, 0, g))
    tile = pl.BlockSpec((1, ATTN_TQ, width), lambda b, g, i: (b, i, g))
    o = pl.pallas_call(
        _sb_attn_kernel,
        grid=(batch, N_HEADS // SB_HEADS_PER_STEP, seq // ATTN_TQ),
        in_specs=[tile, whole, whole],
        out_specs=tile,
        out_shape=jax.ShapeDtypeStruct((batch, seq, MIX_WIDTH), BF16),
        scratch_shapes=[pltpu.VMEM((SB_HEADS_PER_STEP, ATTN_TQ, LANES), F32)] * 2,
        compiler_params=_attn_params(),
        name="sb_attention",
    )(q, k, v)
    return o.reshape(batch * seq, MIX_WIDTH)


def _post_kernel(o_ref, gate_ref, x_ref, p_ref, wout_ref, wg_ref, wp_ref, *rest, with_kv):
    gate = gate_ref[...].astype(F32)
    u = o_ref[...].astype(F32) * (gate * _sigmoid(gate))
    y = x_ref[...] + _dot(u.astype(BF16), wout_ref[...])
    ple = _dot(p_ref[...].astype(BF16), wp_ref[...])
    x_new = y + _sigmoid(_dot(y.astype(BF16), wg_ref[...])) * ple
    if with_kv:
        kvg_ref, wkv_ref, xo_ref, k_ref, v_ref = rest
        kv = _dot(_rms(x_new, kvg_ref[...]).astype(BF16), wkv_ref[...])
        k_ref[...] = kv[:, :MIX_WIDTH].astype(BF16)
        v_ref[...] = kv[:, MIX_WIDTH:].astype(BF16)
    else:
        (xo_ref,) = rest
    xo_ref[...] = x_new


def _layer_spec(rows, cols, layer, col_block=0):
    return pl.BlockSpec((None, rows, cols), lambda i: (layer, 0, col_block))


def _post(o, gate, x, p, layer, w_out, out_layer, w_gate, w_proj, kv_ln_g=None, w_kv=None):
    tokens = x.shape[0]
    with_kv = w_kv is not None
    p_spec = pl.BlockSpec((None, TOKEN_TILE, PLE_DIM), lambda i: (layer, i, 0))
    in_specs = [_row_spec(MIX_WIDTH), _row_spec(MIX_WIDTH), _row_spec(D_MODEL), p_spec,
                _layer_spec(MIX_WIDTH, D_MODEL, out_layer), _layer_spec(D_MODEL, D_MODEL, layer),
                _layer_spec(PLE_DIM, D_MODEL, layer)]
    args = [o, gate, x, p, w_out, w_gate, w_proj]
    out_specs = [_row_spec(D_MODEL)]
    out_shape = [jax.ShapeDtypeStruct((tokens, D_MODEL), F32)]
    if with_kv:
        in_specs += [_const_spec((1, D_MODEL)), _const_spec((D_MODEL, 2 * MIX_WIDTH))]
        args += [kv_ln_g.reshape(1, D_MODEL), w_kv.astype(BF16)]
        out_specs += [_row_spec(MIX_WIDTH), _row_spec(MIX_WIDTH)]
        out_shape += [jax.ShapeDtypeStruct((tokens, MIX_WIDTH), BF16)] * 2
    return pl.pallas_call(
        functools.partial(_post_kernel, with_kv=with_kv),
        grid=(tokens // TOKEN_TILE,),
        in_specs=in_specs, out_specs=out_specs, out_shape=out_shape,
        compiler_params=_params(),
        name="layer_tail_kv" if with_kv else "layer_tail",
    )(*args)


def _sb_front_kernel(x_ref, ln_ref, wq_ref, wgate_ref, q_ref, gate_ref):
    hb = _rms(x_ref[...], ln_ref[...]).astype(BF16)
    q_ref[...] = (_dot(hb, wq_ref[...]) * (HEAD_DIM ** -0.5 * LOG2_E)).astype(BF16)
    gate_ref[...] = _dot(hb, wgate_ref[...]).astype(BF16)


def _sb_front(x, ln_g, w_in, layer):
    tokens = x.shape[0]
    out = jax.ShapeDtypeStruct((tokens, MIX_WIDTH), BF16)
    return pl.pallas_call(
        _sb_front_kernel,
        grid=(tokens // TOKEN_TILE,),
        in_specs=[_row_spec(D_MODEL), _const_spec((1, D_MODEL)),
                  _layer_spec(D_MODEL, MIX_WIDTH, layer, 0), _layer_spec(D_MODEL, MIX_WIDTH, layer, 1)],
        out_specs=[_row_spec(MIX_WIDTH), _row_spec(MIX_WIDTH)],
        out_shape=[out, out],
        compiler_params=_params(),
        name="sb_front",
    )(x, ln_g.reshape(1, D_MODEL), w_in, w_in)


def kernel(x, p, positions, mla_ln_g, mla_w_in, mla_q_norm_g, mla_kv_norm_g, mla_w_q_up, mla_w_kv_up, mla_q_head_g, mla_k_head_g, mla_w_out, kv_ln_g, w_kv_shared, sb_ln_g, sb_w_in, sb_w_out, ple_w_proj, ple_w_gate):
    batch, seq, _ = x.shape
    tokens = batch * seq
    x = x.reshape(tokens, D_MODEL)
    p = p.reshape(DEPTH, tokens, PLE_DIM)
    cos_t, sin_t = _rope_tables(positions)
    mla_w_out, sb_w_out, sb_w_in = mla_w_out.astype(BF16), sb_w_out.astype(BF16), sb_w_in.astype(BF16)
    ple_w_gate, ple_w_proj = ple_w_gate.astype(BF16), ple_w_proj.astype(BF16)
    k_sh = v_sh = None
    for i in range(DEPTH):
        if i < N_A:
            q, k, v, gate = _mla_front(x, mla_ln_g[i], mla_w_in[i], mla_q_norm_g[i], mla_kv_norm_g[i],
                                       mla_w_q_up[i], mla_w_kv_up[i], mla_q_head_g[i], mla_k_head_g[i],
                                       cos_t, sin_t)
            o = _mla_attention(q, k, v, batch, seq)
            w_out, out_layer = mla_w_out, i
        else:
            q, gate = _sb_front(x, sb_ln_g[i - N_A], sb_w_in, i - N_A)
            o = _sb_attention(q, k_sh, v_sh, batch, seq)
            w_out, out_layer = sb_w_out, i - N_A
        if i == N_A - 1:
            x, k_sh, v_sh = _post(o, gate, x, p, i, w_out, out_layer, ple_w_gate, ple_w_proj, kv_ln_g, w_kv_shared)
        else:
            (x,) = _post(o, gate, x, p, i, w_out, out_layer, ple_w_gate, ple_w_proj)
    return x.reshape(batch, seq, D_MODEL)
```

```python
import functools

import jax
import jax.numpy as jnp
from jax import lax
from jax.experimental import pallas as pl
from jax.experimental.pallas import tpu as pltpu

D_MODEL = 1024
DEPTH = 4
N_A = DEPTH // 2
PLE_DIM = 256
N_HEADS = 16
HEAD_DIM = 64
ROPE_DIM = 32
ROPE_HALF = ROPE_DIM // 2
QK_DIM = HEAD_DIM + ROPE_DIM
Q_LORA = 384
KV_LORA = 256
MIX_WIDTH = N_HEADS * HEAD_DIM
ROPE_THETA = 10000.0
EPS = 1e-6

LANES = 128
HEADS_PER_BLOCK = LANES // HEAD_DIM
N_PAIRS = N_HEADS // HEADS_PER_BLOCK
MLA_PAD = N_HEADS * LANES
TOKEN_TILE = 512
FRONT_ROWS = 256
ATTN_TQ = 256
KEY_BLOCK = 512
MLA_HEADS_PER_STEP = 8
SB_HEADS_PER_STEP = 16
SB_SKIP_BELOW = -200.0
LOG2_E = 1.4426950408889634
VMEM_LIMIT = 48 * 1024 * 1024

F32 = jnp.float32
BF16 = jnp.bfloat16


def _dot(a, b):
    return jnp.dot(a, b, preferred_element_type=F32)


def _dot_nt(a, b):
    return lax.dot_general(a, b, (((1,), (1,)), ((), ())), preferred_element_type=F32)


def _rms(x, g):
    return x * lax.rsqrt(jnp.mean(x * x, axis=-1, keepdims=True) + EPS) * g


def _sigmoid(x):
    return 1.0 / (1.0 + jnp.exp(-x))


def _params():
    return pltpu.CompilerParams(dimension_semantics=("arbitrary",), vmem_limit_bytes=VMEM_LIMIT)


def _const_spec(shape):
    return pl.BlockSpec(shape, lambda i: (0,) * len(shape))


def _row_spec(width, tile=TOKEN_TILE):
    return pl.BlockSpec((tile, width), lambda i: (i, 0))


def _rope_table_kernel(pos_ref, inv_ref, sign_ref, keep_ref, cos_ref, sin_ref):
    ang = pos_ref[...] * inv_ref[...]
    cos_ref[...] = jnp.cos(ang) * keep_ref[...]
    sin_ref[...] = jnp.sin(ang) * sign_ref[...]


def _head_block_row(nope, first, second):
    pad = jnp.zeros((LANES - QK_DIM - ROPE_HALF,), F32)
    return jnp.concatenate([nope, first, second, first, pad]).reshape(1, LANES)


def _rope_tables(positions):
    tokens = positions.size
    pos = positions.astype(F32).reshape(tokens, 1)
    inv = 1.0 / (ROPE_THETA ** (jnp.arange(ROPE_HALF, dtype=F32) / ROPE_HALF))
    zeros = jnp.zeros((HEAD_DIM,), F32)
    ones = jnp.ones((ROPE_HALF,), F32)
    inv_row = _head_block_row(zeros, inv, inv)
    sign_row = _head_block_row(zeros, -ones, ones) * _keep_row()
    out = jax.ShapeDtypeStruct((tokens, LANES), F32)
    return pl.pallas_call(
        _rope_table_kernel,
        grid=(tokens // TOKEN_TILE,),
        in_specs=[_row_spec(1), _const_spec((1, LANES)), _const_spec((1, LANES)), _const_spec((1, LANES))],
        out_specs=[_row_spec(LANES), _row_spec(LANES)],
        out_shape=[out, out],
        compiler_params=_params(),
        name="rope_tables",
    )(pos, inv_row, sign_row, _keep_row())


def _keep_row():
    return (jnp.arange(LANES) < QK_DIM).astype(F32).reshape(1, LANES)


def _mla_front_kernel(x_ref, ln_ref, wcq_ref, wckv_ref, wgate_ref, qn_ref, kvn_ref,
                      wq_ref, wk_ref, wv_ref, vone_ref, gq_ref, gk_ref, keep_ref, cos_ref, sin_ref,
                      q_ref, k_ref, v_ref, gate_ref):
    keep = keep_ref[...]
    gq = gq_ref[...]
    gk = gk_ref[...]
    pair_cols = [slice(p * 2 * LANES, (p + 1) * 2 * LANES) for p in range(N_PAIRS)]
    head_blocks = [(p, slice(j * LANES, (j + 1) * LANES))
                   for p in range(N_PAIRS) for j in range(HEADS_PER_BLOCK)]
    ones_row = lax.broadcasted_iota(jnp.int32, (2 * LANES, 2 * LANES), 0)
    ones_col = lax.broadcasted_iota(jnp.int32, (2 * LANES, 2 * LANES), 1)
    same_block = ones_row // LANES == ones_col // LANES
    block_ones_k = jnp.where(same_block, 1.0, 0.0).astype(BF16)
    real_lane = jnp.bitwise_and(ones_row, LANES - 1) < QK_DIM
    block_ones_q = jnp.where(jnp.logical_and(same_block, real_lane), 1.0, 0.0).astype(BF16)

    def project(rows):
        hb = _rms(x_ref[rows, :], ln_ref[...]).astype(BF16)
        cq_kr = _dot(hb, wcq_ref[...])
        cq = cq_kr[:, :Q_LORA]
        kr = cq_kr[:, Q_LORA:]
        ckv = _dot(hb, wckv_ref[...])
        gate_ref[rows, :] = _dot(hb, wgate_ref[...]).astype(BF16)
        cqn = _rms(cq, qn_ref[...]).astype(BF16)
        ckvn = _rms(ckv, kvn_ref[...]).astype(BF16)
        v_ref[rows, :] = (_dot(ckvn, wv_ref[...]) + vone_ref[...]).astype(BF16)
        kn2 = [_dot(ckvn, wk_ref[:, c]) for c in pair_cols]
        qh2 = [_dot(cqn, wq_ref[:, c]) for c in pair_cols]
        return kr, kn2, qh2

    def finish(rows, kr, kn2, qh2):
        cos = cos_ref[rows, :]
        sin = sin_ref[rows, :]

        def rope(y):
            return y * cos + pltpu.roll(y, LANES - ROPE_HALF, 1) * sin

        kns = [kn2[p][:, blk] for p, blk in head_blocks]
        qhs = [qh2[p][:, blk] for p, blk in head_blocks]
        ss_kr = jnp.sum(kr * kr * keep, axis=-1, keepdims=True) + QK_DIM * EPS
        ss_k2 = [_dot((kn * kn).astype(BF16), block_ones_k) for kn in kn2]
        ss_q2 = [_dot((qh * qh).astype(BF16), block_ones_q) for qh in qh2]
        r_k = [lax.rsqrt(ss_k2[p][:, blk] + ss_kr) for p, blk in head_blocks]
        r_q = [lax.rsqrt(ss_q2[p][:, blk] + QK_DIM * EPS) for p, blk in head_blocks]
        k_rope = rope(kr * gk)
        q_rot = [rope(qh * gq) for qh in qhs]
        for h in range(N_HEADS):
            out = slice(h * LANES, (h + 1) * LANES)
            k_ref[rows, out] = ((kns[h] * gk + k_rope) * r_k[h]).astype(BF16)
            q_ref[rows, out] = (q_rot[h] * r_q[h]).astype(BF16)

    groups = [slice(r, r + FRONT_ROWS) for r in range(0, TOKEN_TILE, FRONT_ROWS)]
    projected = [project(rows) for rows in groups]
    for rows, raw in zip(groups, projected):
        finish(rows, *raw)


def _rotary_block(w):
    first = w[..., HEAD_DIM:HEAD_DIM + ROPE_HALF]
    pad = jnp.zeros(w.shape[:-1] + (LANES - QK_DIM - ROPE_HALF,), w.dtype)
    return jnp.concatenate([w, first, pad], axis=-1)


def _pad_heads(w, width):
    k = w.shape[0]
    w = w.reshape(k, N_HEADS, width)
    w = jnp.pad(w, ((0, 0), (0, 0), (0, LANES - width)))
    return w.reshape(k, MLA_PAD)


def _mla_front(x, ln_g, w_in, qn_g, kvn_g, w_q_up, w_kv_up, q_head_g, k_head_g, cos_t, sin_t):
    tokens = x.shape[0]
    wckv = w_in[:, Q_LORA:Q_LORA + KV_LORA].astype(BF16)
    wkr = w_in[:, Q_LORA + KV_LORA:Q_LORA + KV_LORA + ROPE_DIM]
    wkr = _rotary_block(jnp.pad(wkr, ((0, 0), (HEAD_DIM, 0))))
    wcq = jnp.concatenate([w_in[:, :Q_LORA], wkr], axis=1).astype(BF16)
    wgate = w_in[:, Q_LORA + KV_LORA + ROPE_DIM:].astype(BF16)
    wq = _rotary_block(w_q_up.reshape(Q_LORA, N_HEADS, QK_DIM)).reshape(Q_LORA, MLA_PAD).astype(BF16)
    wkv = w_kv_up.reshape(KV_LORA, N_HEADS, 2 * HEAD_DIM)
    wk = _pad_heads(wkv[:, :, :HEAD_DIM].reshape(KV_LORA, MIX_WIDTH), HEAD_DIM).astype(BF16)
    wvh = wkv[:, :, HEAD_DIM:]
    odd_head = (jnp.arange(N_HEADS) % HEADS_PER_BLOCK == 1)[None, :, None]
    wv = jnp.where(odd_head, jnp.concatenate([jnp.zeros_like(wvh), wvh], axis=-1),
                   jnp.concatenate([wvh, jnp.zeros_like(wvh)], axis=-1)).reshape(KV_LORA, MLA_PAD).astype(BF16)
    value_lane = (jnp.arange(LANES) < HEAD_DIM)[None, :] != odd_head[0]
    v_one = jnp.where(value_lane, 0.0, 1.0).astype(F32).reshape(1, MLA_PAD)
    k_fold = QK_DIM ** 0.5
    q_fold = LOG2_E
    out = lambda w: jax.ShapeDtypeStruct((tokens, w), BF16)
    return pl.pallas_call(
        _mla_front_kernel,
        grid=(tokens // TOKEN_TILE,),
        in_specs=[_row_spec(D_MODEL), _const_spec((1, D_MODEL)),
                  _const_spec((D_MODEL, Q_LORA + LANES)), _const_spec((D_MODEL, KV_LORA)),
                  _const_spec((D_MODEL, MIX_WIDTH)),
                  _const_spec((1, Q_LORA)), _const_spec((1, KV_LORA)),
                  _const_spec((Q_LORA, MLA_PAD)), _const_spec((KV_LORA, MLA_PAD)),
                  _const_spec((KV_LORA, MLA_PAD)), _const_spec((1, MLA_PAD)),
                  _const_spec((1, LANES)), _const_spec((1, LANES)), _const_spec((1, LANES)),
                  _row_spec(LANES), _row_spec(LANES)],
        out_specs=[_row_spec(MLA_PAD), _row_spec(MLA_PAD), _row_spec(MLA_PAD), _row_spec(MIX_WIDTH)],
        out_shape=[out(MLA_PAD), out(MLA_PAD), out(MLA_PAD), out(MIX_WIDTH)],
        compiler_params=_params(),
        name="mla_front",
    )(x, ln_g.reshape(1, D_MODEL), wcq, wckv, wgate, qn_g.reshape(1, Q_LORA),
      kvn_g.reshape(1, KV_LORA), wq, wk, wv, v_one, _rotary_block(q_head_g * q_fold).reshape(1, LANES),
      _rotary_block(k_head_g * k_fold).reshape(1, LANES), _keep_row(), cos_t, sin_t)


def _attn_params():
    return pltpu.CompilerParams(dimension_semantics=("arbitrary",) * 3, vmem_limit_bytes=VMEM_LIMIT)


def _first_head_lanes(shape):
    return lax.broadcasted_iota(jnp.int32, shape, 1) < HEAD_DIM


def _is_odd(i):
    return jnp.bitwise_and(i, 1) == 1


def _widen(x, width):
    return jnp.concatenate([x] * (width // LANES), axis=1)


def _mla_attn_kernel(q_ref, k_ref, v_ref, o_ref, m_scr, acc_scr):
    qi = pl.program_id(2)
    n_wide = qi // (KEY_BLOCK // ATTN_TQ)
    row = lax.broadcasted_iota(jnp.int32, (ATTN_TQ, ATTN_TQ), 0)
    col = lax.broadcasted_iota(jnp.int32, (ATTN_TQ, ATTN_TQ), 1)
    causal = col <= row
    heads = range(MLA_HEADS_PER_STEP)
    qs = [q_ref[0, :, h * LANES:(h + 1) * LANES] for h in heads]

    def step(start, width, visible=None, fresh=False):
        scores = [_dot_nt(qs[h], k_ref[0, pl.ds(start, width), h * LANES:(h + 1) * LANES])
                  for h in heads]
        for h in heads:
            s = scores[h] if visible is None else jnp.where(visible, scores[h], -jnp.inf)
            block_max = jnp.broadcast_to(jnp.max(s, axis=-1, keepdims=True), (ATTN_TQ, LANES))
            m_new = block_max if fresh else jnp.maximum(m_scr[h], block_max)
            p = jnp.exp2(s - _widen(m_new, width))
            alpha = None if fresh else jnp.exp2(m_scr[h] - m_new)
            m_scr[h] = m_new
            pv = _dot(p.astype(BF16), v_ref[0, pl.ds(start, width), h * LANES:(h + 1) * LANES])
            acc_scr[h] = pv if fresh else alpha * acc_scr[h] + pv

    @pl.when(jnp.logical_not(_is_odd(qi)))
    def _():
        step(pl.multiple_of(qi * ATTN_TQ, ATTN_TQ), ATTN_TQ, causal, fresh=True)

    @pl.when(_is_odd(qi))
    def _():
        wide_row = lax.broadcasted_iota(jnp.int32, (ATTN_TQ, KEY_BLOCK), 0)
        wide_col = lax.broadcasted_iota(jnp.int32, (ATTN_TQ, KEY_BLOCK), 1)
        step(pl.multiple_of(n_wide * KEY_BLOCK, KEY_BLOCK), KEY_BLOCK,
             wide_col <= wide_row + (KEY_BLOCK - ATTN_TQ), fresh=True)

    @pl.loop(0, n_wide)
    def _(kb):
        step(pl.multiple_of(kb * KEY_BLOCK, KEY_BLOCK), KEY_BLOCK)

    first = _first_head_lanes((ATTN_TQ, LANES))
    for b in range(MLA_HEADS_PER_STEP // HEADS_PER_BLOCK):
        acc0, acc1 = acc_scr[HEADS_PER_BLOCK * b], acc_scr[HEADS_PER_BLOCK * b + 1]
        numerator = jnp.where(first, acc0, acc1)
        denominator = pltpu.roll(jnp.where(first, acc1, acc0), HEAD_DIM, 1)
        o_ref[0, :, b * LANES:(b + 1) * LANES] = (numerator / denominator).astype(BF16)


def _mla_attention(q, k, v, batch, seq):
    q = q.reshape(batch, seq, MLA_PAD)
    k = k.reshape(batch, seq, MLA_PAD)
    v = v.reshape(batch, seq, MLA_PAD)
    qk_w = MLA_HEADS_PER_STEP * LANES
    v_w = MLA_HEADS_PER_STEP * HEAD_DIM
    o = pl.pallas_call(
        _mla_attn_kernel,
        grid=(batch, N_HEADS // MLA_HEADS_PER_STEP, seq // ATTN_TQ),
        in_specs=[pl.BlockSpec((1, ATTN_TQ, qk_w), lambda b, g, i: (b, i, g)),
                  pl.BlockSpec((1, seq, qk_w), lambda b, g, i: (b, 0, g)),
                  pl.BlockSpec((1, seq, qk_w), lambda b, g, i: (b, 0, g))],
        out_specs=pl.BlockSpec((1, ATTN_TQ, v_w), lambda b, g, i: (b, i, g)),
        out_shape=jax.ShapeDtypeStruct((batch, seq, MIX_WIDTH), BF16),
        scratch_shapes=[pltpu.VMEM((MLA_HEADS_PER_STEP, ATTN_TQ, LANES), F32)] * 2,
        compiler_params=_attn_params(),
        name="mla_attention",
    )(q, k, v)
    return o.reshape(batch * seq, MIX_WIDTH)


def _sb_attn_kernel(q_ref, k_ref, v_ref, o_ref, later_scr, acc_scr):
    qi = pl.program_id(2)
    t = ATTN_TQ
    row = lax.broadcasted_iota(jnp.int32, (t, t), 0)
    col = lax.broadcasted_iota(jnp.int32, (t, t), 1)
    strict = col < row
    suffix = jnp.where(row > col, 1.0, 0.0).astype(BF16)
    suffix_and_total = jnp.concatenate([suffix, jnp.ones((t, LANES), BF16)], axis=1)
    first = _first_head_lanes((t, LANES))
    heads = range(SB_HEADS_PER_STEP)
    lanes = [slice(h // HEADS_PER_BLOCK * LANES, (h // HEADS_PER_BLOCK + 1) * LANES) for h in heads]
    qs = []
    for h in heads:
        q2 = q_ref[0, :, lanes[h]]
        mine = first if h % HEADS_PER_BLOCK == 0 else jnp.logical_not(first)
        qs.append(jnp.where(mine, q2, jnp.zeros_like(q2)))


    def gate_logs(z, visible=None):
        soft = jnp.log(1.0 + jnp.exp2(-jnp.abs(z))) * LOG2_E
        log_beta = jnp.minimum(z, 0.0) - soft
        log_rest = log_beta - z
        if visible is not None:
            log_rest = jnp.where(visible, log_rest, 0.0)
        return log_beta, log_rest

    def row_sums(x):
        return jnp.broadcast_to(jnp.sum(x, axis=-1, keepdims=True), (x.shape[0], LANES))

    def diagonal_block():
        half = t // 2
        parts = [(slice(0, half), half), (slice(half, t), t)]
        chains = [(h, rows, width) for h in heads for rows, width in parts]
        zs = [_dot_nt(qs[h][rows], k_ref[0, pl.ds(diag, width), lanes[h]]) for h, rows, width in chains]
        logs = [gate_logs(z, strict[rows, :width]) for z, (_, rows, width) in zip(zs, chains)]
        sums = [_dot(log_rest.astype(BF16), suffix[:width, :width])
                for (_, log_rest), (_, _, width) in zip(logs, chains)]
        weights = [jnp.where(strict[rows, :width], jnp.exp2(log_beta + s), 0.0).astype(BF16)
                   for (log_beta, _), s, (_, rows, width) in zip(logs, sums, chains)]
        highest = None
        for h in heads:
            (_, _, w0), (_, _, w1) = chains[2 * h], chains[2 * h + 1]
            later = jnp.concatenate([row_sums(logs[2 * h][1]), row_sums(logs[2 * h + 1][1])], axis=0)
            later_scr[h] = later
            acc_scr[h] = jnp.concatenate(
                [_dot(weights[2 * h], v_ref[0, pl.ds(diag, w0), lanes[h]]),
                 _dot(weights[2 * h + 1], v_ref[0, pl.ds(diag, w1), lanes[h]])], axis=0)
            top = jnp.max(later)
            highest = top if highest is None else jnp.maximum(highest, top)
        return highest

    def earlier_block(start):
        zs = [_dot_nt(qs[h], k_ref[0, pl.ds(start, t), lanes[h]]) for h in heads]
        logs = [gate_logs(z) for z in zs]
        sums = [_dot(log_rest.astype(BF16), suffix_and_total) for _, log_rest in logs]
        highest = None
        weights = []
        for h in heads:
            later = later_scr[h]
            weights.append(jnp.exp2(logs[h][0] + sums[h][:, :t] + _widen(later, t)).astype(BF16))
            later = later + sums[h][:, t:]
            later_scr[h] = later
            top = jnp.max(later)
            highest = top if highest is None else jnp.maximum(highest, top)
        for h in heads:
            acc_scr[h] += _dot(weights[h], v_ref[0, pl.ds(start, t), lanes[h]])
        return highest

    diag = pl.multiple_of(qi * t, t)
    highest_later = diagonal_block()

    def more(c):
        blocks_done, highest = c
        return jnp.logical_and(blocks_done < qi, highest > SB_SKIP_BELOW)

    def one_block(c):
        blocks_done, _ = c
        return blocks_done + 1, earlier_block(pl.multiple_of(diag - (blocks_done + 1) * t, t))

    lax.while_loop(more, one_block, (jnp.int32(0), highest_later))

    for b in range(SB_HEADS_PER_STEP // HEADS_PER_BLOCK):
        o_ref[0, :, b * LANES:(b + 1) * LANES] = jnp.where(
            first, acc_scr[HEADS_PER_BLOCK * b], acc_scr[HEADS_PER_BLOCK * b + 1]).astype(BF16)


def _sb_attention(q, k, v, batch, seq):
    q = q.reshape(batch, seq, MIX_WIDTH)
    k = k.reshape(batch, seq, MIX_WIDTH)
    v = v.reshape(batch, seq, MIX_WIDTH)
    width = SB_HEADS_PER_STEP * HEAD_DIM
    whole = pl.BlockSpec((1, seq, width), lambda b, g, i: (b, 0, g))
    tile = pl.BlockSpec((1, ATTN_TQ, width), lambda b, g, i: (b, i, g))
    o = pl.pallas_call(
        _sb_attn_kernel,
        grid=(batch, N_HEADS // SB_HEADS_PER_STEP, seq // ATTN_TQ),
        in_specs=[tile, whole, whole],
        out_specs=tile,
        out_shape=jax.ShapeDtypeStruct((batch, seq, MIX_WIDTH), BF16),
        scratch_shapes=[pltpu.VMEM((SB_HEADS_PER_STEP, ATTN_TQ, LANES), F32)] * 2,
        compiler_params=_attn_params(),
        name="sb_attention",
    )(q, k, v)
    return o.reshape(batch * seq, MIX_WIDTH)


def _post_kernel(o_ref, gate_ref, x_ref, p_ref, wout_ref, wg_ref, wp_ref, *rest, with_kv):
    gate = gate_ref[...].astype(F32)
    u = o_ref[...].astype(F32) * (gate * _sigmoid(gate))
    y = x_ref[...] + _dot(u.astype(BF16), wout_ref[...])
    ple = _dot(p_ref[...].astype(BF16), wp_ref[...])
    x_new = y + _sigmoid(_dot(y.astype(BF16), wg_ref[...])) * ple
    if with_kv:
        kvg_ref, wkv_ref, xo_ref, k_ref, v_ref = rest
        kv = _dot(_rms(x_new, kvg_ref[...]).astype(BF16), wkv_ref[...])
        k_ref[...] = kv[:, :MIX_WIDTH].astype(BF16)
        v_ref[...] = kv[:, MIX_WIDTH:].astype(BF16)
    else:
        (xo_ref,) = rest
    xo_ref[...] = x_new


def _layer_spec(rows, cols, layer, col_block=0):
    return pl.BlockSpec((None, rows, cols), lambda i: (layer, 0, col_block))


def _post(o, gate, x, p, layer, w_out, out_layer, w_gate, w_proj, kv_ln_g=None, w_kv=None):
    tokens = x.shape[0]
    with_kv = w_kv is not None
    p_spec = pl.BlockSpec((None, TOKEN_TILE, PLE_DIM), lambda i: (layer, i, 0))
    in_specs = [_row_spec(MIX_WIDTH), _row_spec(MIX_WIDTH), _row_spec(D_MODEL), p_spec,
                _layer_spec(MIX_WIDTH, D_MODEL, out_layer), _layer_spec(D_MODEL, D_MODEL, layer),
                _layer_spec(PLE_DIM, D_MODEL, layer)]
    args = [o, gate, x, p, w_out, w_gate, w_proj]
    out_specs = [_row_spec(D_MODEL)]
    out_shape = [jax.ShapeDtypeStruct((tokens, D_MODEL), F32)]
    if with_kv:
        in_specs += [_const_spec((1, D_MODEL)), _const_spec((D_MODEL, 2 * MIX_WIDTH))]
        args += [kv_ln_g.reshape(1, D_MODEL), w_kv.astype(BF16)]
        out_specs += [_row_spec(MIX_WIDTH), _row_spec(MIX_WIDTH)]
        out_shape += [jax.ShapeDtypeStruct((tokens, MIX_WIDTH), BF16)] * 2
    return pl.pallas_call(
        functools.partial(_post_kernel, with_kv=with_kv),
        grid=(tokens // TOKEN_TILE,),
        in_specs=in_specs, out_specs=out_specs, out_shape=out_shape,
        compiler_params=_params(),
        name="layer_tail_kv" if with_kv else "layer_tail",
    )(*args)


def _sb_front_kernel(x_ref, ln_ref, wq_ref, wgate_ref, q_ref, gate_ref):
    hb = _rms(x_ref[...], ln_ref[...]).astype(BF16)
    q_ref[...] = (_dot(hb, wq_ref[...]) * (HEAD_DIM ** -0.5 * LOG2_E)).astype(BF16)
    gate_ref[...] = _dot(hb, wgate_ref[...]).astype(BF16)


def _sb_front(x, ln_g, w_in, layer):
    tokens = x.shape[0]
    out = jax.ShapeDtypeStruct((tokens, MIX_WIDTH), BF16)
    return pl.pallas_call(
        _sb_front_kernel,
        grid=(tokens // TOKEN_TILE,),
        in_specs=[_row_spec(D_MODEL), _const_spec((1, D_MODEL)),
                  _layer_spec(D_MODEL, MIX_WIDTH, layer, 0), _layer_spec(D_MODEL, MIX_WIDTH, layer, 1)],
        out_specs=[_row_spec(MIX_WIDTH), _row_spec(MIX_WIDTH)],
        out_shape=[out, out],
        compiler_params=_params(),
        name="sb_front",
    )(x, ln_g.reshape(1, D_MODEL), w_in, w_in)


def kernel(x, p, positions, mla_ln_g, mla_w_in, mla_q_norm_g, mla_kv_norm_g, mla_w_q_up, mla_w_kv_up, mla_q_head_g, mla_k_head_g, mla_w_out, kv_ln_g, w_kv_shared, sb_ln_g, sb_w_in, sb_w_out, ple_w_proj, ple_w_gate):
    batch, seq, _ = x.shape
    tokens = batch * seq
    x = x.reshape(tokens, D_MODEL)
    p = p.reshape(DEPTH, tokens, PLE_DIM)
    cos_t, sin_t = _rope_tables(positions)
    mla_w_out, sb_w_out, sb_w_in = mla_w_out.astype(BF16), sb_w_out.astype(BF16), sb_w_in.astype(BF16)
    ple_w_gate, ple_w_proj = ple_w_gate.astype(BF16), ple_w_proj.astype(BF16)
    k_sh = v_sh = None
    for i in range(DEPTH):
        if i < N_A:
            q, k, v, gate = _mla_front(x, mla_ln_g[i], mla_w_in[i], mla_q_norm_g[i], mla_kv_norm_g[i],
                                       mla_w_q_up[i], mla_w_kv_up[i], mla_q_head_g[i], mla_k_head_g[i],
                                       cos_t, sin_t)
            o = _mla_attention(q, k, v, batch, seq)
            w_out, out_layer = mla_w_out, i
        else:
            q, gate = _sb_front(x, sb_ln_g[i - N_A], sb_w_in, i - N_A)
            o = _sb_attention(q, k_sh, v_sh, batch, seq)
            w_out, out_layer = sb_w_out, i - N_A
        if i == N_A - 1:
            x, k_sh, v_sh = _post(o, gate, x, p, i, w_out, out_layer, ple_w_gate, ple_w_proj, kv_ln_g, w_kv_shared)
        else:
            (x,) = _post(o, gate, x, p, i, w_out, out_layer, ple_w_gate, ple_w_proj)
    return x.reshape(batch, seq, D_MODEL)
```

```python
import functools

import jax
import jax.numpy as jnp
from jax import lax
from jax.experimental import pallas as pl
from jax.experimental.pallas import tpu as pltpu

D_MODEL = 1024
DEPTH = 4
N_A = DEPTH // 2
PLE_DIM = 256
N_HEADS = 16
HEAD_DIM = 64
ROPE_DIM = 32
ROPE_HALF = ROPE_DIM // 2
QK_DIM = HEAD_DIM + ROPE_DIM
Q_LORA = 384
KV_LORA = 256
MIX_WIDTH = N_HEADS * HEAD_DIM
ROPE_THETA = 10000.0
EPS = 1e-6

LANES = 128
HEADS_PER_BLOCK = LANES // HEAD_DIM
N_PAIRS = N_HEADS // HEADS_PER_BLOCK
MLA_PAD = N_HEADS * LANES
TOKEN_TILE = 512
FRONT_ROWS = 256
POST_ROWS = 256
ATTN_TQ = 256
KEY_BLOCK = 512
MLA_HEADS_PER_STEP = 8
SB_HEADS_PER_STEP = 16
SB_SKIP_BELOW = -200.0
LOG2_E = 1.4426950408889634
VMEM_LIMIT = 48 * 1024 * 1024

F32 = jnp.float32
BF16 = jnp.bfloat16


def _dot(a, b):
    return jnp.dot(a, b, preferred_element_type=F32)


def _dot_nt(a, b):
    return lax.dot_general(a, b, (((1,), (1,)), ((), ())), preferred_element_type=F32)


def _rms(x, g):
    return x * lax.rsqrt(jnp.mean(x * x, axis=-1, keepdims=True) + EPS) * g


def _sigmoid(x):
    return 1.0 / (1.0 + jnp.exp(-x))


def _params():
    return pltpu.CompilerParams(dimension_semantics=("arbitrary",), vmem_limit_bytes=VMEM_LIMIT)


def _const_spec(shape):
    return pl.BlockSpec(shape, lambda i: (0,) * len(shape))


def _row_spec(width, tile=TOKEN_TILE):
    return pl.BlockSpec((tile, width), lambda i: (i, 0))


def _rope_table_kernel(pos_ref, inv_ref, sign_ref, keep_ref, cos_ref, sin_ref):
    ang = pos_ref[...] * inv_ref[...]
    cos_ref[...] = jnp.cos(ang) * keep_ref[...]
    sin_ref[...] = jnp.sin(ang) * sign_ref[...]


def _head_block_row(nope, first, second):
    pad = jnp.zeros((LANES - QK_DIM - ROPE_HALF,), F32)
    return jnp.concatenate([nope, first, second, first, pad]).reshape(1, LANES)


def _rope_tables(positions):
    tokens = positions.size
    pos = positions.astype(F32).reshape(tokens, 1)
    inv = 1.0 / (ROPE_THETA ** (jnp.arange(ROPE_HALF, dtype=F32) / ROPE_HALF))
    zeros = jnp.zeros((HEAD_DIM,), F32)
    ones = jnp.ones((ROPE_HALF,), F32)
    inv_row = _head_block_row(zeros, inv, inv)
    sign_row = _head_block_row(zeros, -ones, ones) * _keep_row()
    out = jax.ShapeDtypeStruct((tokens, LANES), F32)
    return pl.pallas_call(
        _rope_table_kernel,
        grid=(tokens // TOKEN_TILE,),
        in_specs=[_row_spec(1), _const_spec((1, LANES)), _const_spec((1, LANES)), _const_spec((1, LANES))],
        out_specs=[_row_spec(LANES), _row_spec(LANES)],
        out_shape=[out, out],
        compiler_params=_params(),
        name="rope_tables",
    )(pos, inv_row, sign_row, _keep_row())


def _keep_row():
    return (jnp.arange(LANES) < QK_DIM).astype(F32).reshape(1, LANES)


def _mla_front_kernel(x_ref, ln_ref, wcq_ref, wckv_ref, wgate_ref, qn_ref, kvn_ref,
                      wq_ref, wk_ref, wv_ref, vone_ref, gq_ref, gk_ref, keep_ref, cos_ref, sin_ref,
                      q_ref, k_ref, v_ref, gate_ref):
    keep = keep_ref[...]
    gq = gq_ref[...]
    gk = gk_ref[...]
    pair_cols = [slice(p * 2 * LANES, (p + 1) * 2 * LANES) for p in range(N_PAIRS)]
    head_blocks = [(p, slice(j * LANES, (j + 1) * LANES))
                   for p in range(N_PAIRS) for j in range(HEADS_PER_BLOCK)]
    ones_row = lax.broadcasted_iota(jnp.int32, (2 * LANES, 2 * LANES), 0)
    ones_col = lax.broadcasted_iota(jnp.int32, (2 * LANES, 2 * LANES), 1)
    same_block = ones_row // LANES == ones_col // LANES
    block_ones_k = jnp.where(same_block, 1.0, 0.0).astype(BF16)
    real_lane = jnp.bitwise_and(ones_row, LANES - 1) < QK_DIM
    block_ones_q = jnp.where(jnp.logical_and(same_block, real_lane), 1.0, 0.0).astype(BF16)

    def project(rows):
        hb = _rms(x_ref[rows, :], ln_ref[...]).astype(BF16)
        cq_kr = _dot(hb, wcq_ref[...])
        cq = cq_kr[:, :Q_LORA]
        kr = cq_kr[:, Q_LORA:]
        ckv = _dot(hb, wckv_ref[...])
        gate_ref[rows, :] = _dot(hb, wgate_ref[...]).astype(BF16)
        cqn = _rms(cq, qn_ref[...]).astype(BF16)
        ckvn = _rms(ckv, kvn_ref[...]).astype(BF16)
        v_ref[rows, :] = (_dot(ckvn, wv_ref[...]) + vone_ref[...]).astype(BF16)
        kn2 = [_dot(ckvn, wk_ref[:, c]) for c in pair_cols]
        qh2 = [_dot(cqn, wq_ref[:, c]) for c in pair_cols]
        return kr, kn2, qh2

    def finish(rows, kr, kn2, qh2):
        cos = cos_ref[rows, :]
        sin = sin_ref[rows, :]

        def rope(y):
            return y * cos + pltpu.roll(y, LANES - ROPE_HALF, 1) * sin

        kns = [kn2[p][:, blk] for p, blk in head_blocks]
        qhs = [qh2[p][:, blk] for p, blk in head_blocks]
        ss_kr = jnp.sum(kr * kr * keep, axis=-1, keepdims=True) + QK_DIM * EPS
        ss_k2 = [_dot((kn * kn).astype(BF16), block_ones_k) for kn in kn2]
        ss_q2 = [_dot((qh * qh).astype(BF16), block_ones_q) for qh in qh2]
        r_k = [lax.rsqrt(ss_k2[p][:, blk] + ss_kr) for p, blk in head_blocks]
        r_q = [lax.rsqrt(ss_q2[p][:, blk] + QK_DIM * EPS) for p, blk in head_blocks]
        k_rope = rope(kr * gk)
        q_rot = [rope(qh * gq) for qh in qhs]
        for h in range(N_HEADS):
            out = slice(h * LANES, (h + 1) * LANES)
            k_ref[rows, out] = ((kns[h] * gk + k_rope) * r_k[h]).astype(BF16)
            q_ref[rows, out] = (q_rot[h] * r_q[h]).astype(BF16)

    groups = [slice(r, r + FRONT_ROWS) for r in range(0, TOKEN_TILE, FRONT_ROWS)]
    projected = [project(rows) for rows in groups]
    for rows, raw in zip(groups, projected):
        finish(rows, *raw)


def _rotary_block(w):
    first = w[..., HEAD_DIM:HEAD_DIM + ROPE_HALF]
    pad = jnp.zeros(w.shape[:-1] + (LANES - QK_DIM - ROPE_HALF,), w.dtype)
    return jnp.concatenate([w, first, pad], axis=-1)


def _pad_heads(w, width):
    k = w.shape[0]
    w = w.reshape(k, N_HEADS, width)
    w = jnp.pad(w, ((0, 0), (0, 0), (0, LANES - width)))
    return w.reshape(k, MLA_PAD)


def _mla_front(x, ln_g, w_in, qn_g, kvn_g, w_q_up, w_kv_up, q_head_g, k_head_g, cos_t, sin_t):
    tokens = x.shape[0]
    wckv = w_in[:, Q_LORA:Q_LORA + KV_LORA].astype(BF16)
    wkr = w_in[:, Q_LORA + KV_LORA:Q_LORA + KV_LORA + ROPE_DIM]
    wkr = _rotary_block(jnp.pad(wkr, ((0, 0), (HEAD_DIM, 0))))
    wcq = jnp.concatenate([w_in[:, :Q_LORA], wkr], axis=1).astype(BF16)
    wgate = w_in[:, Q_LORA + KV_LORA + ROPE_DIM:].astype(BF16)
    wq = _rotary_block(w_q_up.reshape(Q_LORA, N_HEADS, QK_DIM)).reshape(Q_LORA, MLA_PAD).astype(BF16)
    wkv = w_kv_up.reshape(KV_LORA, N_HEADS, 2 * HEAD_DIM)
    wk = _pad_heads(wkv[:, :, :HEAD_DIM].reshape(KV_LORA, MIX_WIDTH), HEAD_DIM).astype(BF16)
    wvh = wkv[:, :, HEAD_DIM:]
    odd_head = (jnp.arange(N_HEADS) % HEADS_PER_BLOCK == 1)[None, :, None]
    wv = jnp.where(odd_head, jnp.concatenate([jnp.zeros_like(wvh), wvh], axis=-1),
                   jnp.concatenate([wvh, jnp.zeros_like(wvh)], axis=-1)).reshape(KV_LORA, MLA_PAD).astype(BF16)
    value_lane = (jnp.arange(LANES) < HEAD_DIM)[None, :] != odd_head[0]
    v_one = jnp.where(value_lane, 0.0, 1.0).astype(F32).reshape(1, MLA_PAD)
    k_fold = QK_DIM ** 0.5
    q_fold = LOG2_E
    out = lambda w: jax.ShapeDtypeStruct((tokens, w), BF16)
    return pl.pallas_call(
        _mla_front_kernel,
        grid=(tokens // TOKEN_TILE,),
        in_specs=[_row_spec(D_MODEL), _const_spec((1, D_MODEL)),
                  _const_spec((D_MODEL, Q_LORA + LANES)), _const_spec((D_MODEL, KV_LORA)),
                  _const_spec((D_MODEL, MIX_WIDTH)),
                  _const_spec((1, Q_LORA)), _const_spec((1, KV_LORA)),
                  _const_spec((Q_LORA, MLA_PAD)), _const_spec((KV_LORA, MLA_PAD)),
                  _const_spec((KV_LORA, MLA_PAD)), _const_spec((1, MLA_PAD)),
                  _const_spec((1, LANES)), _const_spec((1, LANES)), _const_spec((1, LANES)),
                  _row_spec(LANES), _row_spec(LANES)],
        out_specs=[_row_spec(MLA_PAD), _row_spec(MLA_PAD), _row_spec(MLA_PAD), _row_spec(MIX_WIDTH)],
        out_shape=[out(MLA_PAD), out(MLA_PAD), out(MLA_PAD), out(MIX_WIDTH)],
        compiler_params=_params(),
        name="mla_front",
    )(x, ln_g.reshape(1, D_MODEL), wcq, wckv, wgate, qn_g.reshape(1, Q_LORA),
      kvn_g.reshape(1, KV_LORA), wq, wk, wv, v_one, _rotary_block(q_head_g * q_fold).reshape(1, LANES),
      _rotary_block(k_head_g * k_fold).reshape(1, LANES), _keep_row(), cos_t, sin_t)


def _attn_params():
    return pltpu.CompilerParams(dimension_semantics=("arbitrary",) * 3, vmem_limit_bytes=VMEM_LIMIT)


def _first_head_lanes(shape):
    return lax.broadcasted_iota(jnp.int32, shape, 1) < HEAD_DIM


def _is_odd(i):
    return jnp.bitwise_and(i, 1) == 1


def _widen(x, width):
    return jnp.concatenate([x] * (width // LANES), axis=1)


def _mla_attn_kernel(q_ref, k_ref, v_ref, o_ref, m_scr, acc_scr):
    qi = pl.program_id(2)
    n_wide = qi // (KEY_BLOCK // ATTN_TQ)
    row = lax.broadcasted_iota(jnp.int32, (ATTN_TQ, ATTN_TQ), 0)
    col = lax.broadcasted_iota(jnp.int32, (ATTN_TQ, ATTN_TQ), 1)
    causal = col <= row
    heads = range(MLA_HEADS_PER_STEP)

    def step(start, width, visible=None, fresh=False):
        scores = [_dot_nt(q_ref[0, :, h * LANES:(h + 1) * LANES],
                          k_ref[0, pl.ds(start, width), h * LANES:(h + 1) * LANES]) for h in heads]
        for h in heads:
            s = scores[h] if visible is None else jnp.where(visible, scores[h], -jnp.inf)
            block_max = jnp.broadcast_to(jnp.max(s, axis=-1, keepdims=True), (ATTN_TQ, LANES))
            m_new = block_max if fresh else jnp.maximum(m_scr[h], block_max)
            p = jnp.exp2(s - _widen(m_new, width))
            alpha = None if fresh else jnp.exp2(m_scr[h] - m_new)
            m_scr[h] = m_new
            pv = _dot(p.astype(BF16), v_ref[0, pl.ds(start, width), h * LANES:(h + 1) * LANES])
            acc_scr[h] = pv if fresh else alpha * acc_scr[h] + pv

    @pl.when(jnp.logical_not(_is_odd(qi)))
    def _():
        step(pl.multiple_of(qi * ATTN_TQ, ATTN_TQ), ATTN_TQ, causal, fresh=True)

    @pl.when(_is_odd(qi))
    def _():
        wide_row = lax.broadcasted_iota(jnp.int32, (ATTN_TQ, KEY_BLOCK), 0)
        wide_col = lax.broadcasted_iota(jnp.int32, (ATTN_TQ, KEY_BLOCK), 1)
        step(pl.multiple_of(n_wide * KEY_BLOCK, KEY_BLOCK), KEY_BLOCK,
             wide_col <= wide_row + (KEY_BLOCK - ATTN_TQ), fresh=True)

    @pl.loop(0, n_wide)
    def _(kb):
        step(pl.multiple_of(kb * KEY_BLOCK, KEY_BLOCK), KEY_BLOCK)

    first = _first_head_lanes((ATTN_TQ, LANES))
    for b in range(MLA_HEADS_PER_STEP // HEADS_PER_BLOCK):
        acc0, acc1 = acc_scr[HEADS_PER_BLOCK * b], acc_scr[HEADS_PER_BLOCK * b + 1]
        numerator = jnp.where(first, acc0, acc1)
        denominator = pltpu.roll(jnp.where(first, acc1, acc0), HEAD_DIM, 1)
        o_ref[0, :, b * LANES:(b + 1) * LANES] = (numerator / denominator).astype(BF16)


def _mla_attention(q, k, v, batch, seq):
    q = q.reshape(batch, seq, MLA_PAD)
    k = k.reshape(batch, seq, MLA_PAD)
    v = v.reshape(batch, seq, MLA_PAD)
    qk_w = MLA_HEADS_PER_STEP * LANES
    v_w = MLA_HEADS_PER_STEP * HEAD_DIM
    o = pl.pallas_call(
        _mla_attn_kernel,
        grid=(batch, N_HEADS // MLA_HEADS_PER_STEP, seq // ATTN_TQ),
        in_specs=[pl.BlockSpec((1, ATTN_TQ, qk_w), lambda b, g, i: (b, i, g)),
                  pl.BlockSpec((1, seq, qk_w), lambda b, g, i: (b, 0, g)),
                  pl.BlockSpec((1, seq, qk_w), lambda b, g, i: (b, 0, g))],
        out_specs=pl.BlockSpec((1, ATTN_TQ, v_w), lambda b, g, i: (b, i, g)),
        out_shape=jax.ShapeDtypeStruct((batch, seq, MIX_WIDTH), BF16),
        scratch_shapes=[pltpu.VMEM((MLA_HEADS_PER_STEP, ATTN_TQ, LANES), F32)] * 2,
        compiler_params=_attn_params(),
        name="mla_attention",
    )(q, k, v)
    return o.reshape(batch * seq, MIX_WIDTH)


def _sb_attn_kernel(q_ref, k_ref, v_ref, o_ref, later_scr, acc_scr):
    qi = pl.program_id(2)
    t = ATTN_TQ
    row = lax.broadcasted_iota(jnp.int32, (t, t), 0)
    col = lax.broadcasted_iota(jnp.int32, (t, t), 1)
    strict = col < row
    suffix = jnp.where(row > col, 1.0, 0.0).astype(BF16)
    suffix_and_total = jnp.concatenate([suffix, jnp.ones((t, LANES), BF16)], axis=1)
    first = _first_head_lanes((t, LANES))
    heads = range(SB_HEADS_PER_STEP)
    lanes = [slice(h // HEADS_PER_BLOCK * LANES, (h // HEADS_PER_BLOCK + 1) * LANES) for h in heads]
    qs = []
    for h in heads:
        q2 = q_ref[0, :, lanes[h]]
        mine = first if h % HEADS_PER_BLOCK == 0 else jnp.logical_not(first)
        qs.append(jnp.where(mine, q2, jnp.zeros_like(q2)))


    def gate_logs(z, visible=None):
        soft = jnp.log(1.0 + jnp.exp2(-jnp.abs(z))) * LOG2_E
        log_beta = jnp.minimum(z, 0.0) - soft
        log_rest = log_beta - z
        if visible is not None:
            log_rest = jnp.where(visible, log_rest, 0.0)
        return log_beta, log_rest

    def row_sums(x):
        return jnp.broadcast_to(jnp.sum(x, axis=-1, keepdims=True), (x.shape[0], LANES))

    def diagonal_block():
        half = t // 2
        parts = [(slice(0, half), half), (slice(half, t), t)]
        chains = [(h, rows, width) for h in heads for rows, width in parts]
        zs = [_dot_nt(qs[h][rows], k_ref[0, pl.ds(diag, width), lanes[h]]) for h, rows, width in chains]
        logs = [gate_logs(z, strict[rows, :width]) for z, (_, rows, width) in zip(zs, chains)]
        sums = [_dot(log_rest.astype(BF16), suffix[:width, :width])
                for (_, log_rest), (_, _, width) in zip(logs, chains)]
        weights = [jnp.where(strict[rows, :width], jnp.exp2(log_beta + s), 0.0).astype(BF16)
                   for (log_beta, _), s, (_, rows, width) in zip(logs, sums, chains)]
        highest = None
        for h in heads:
            (_, _, w0), (_, _, w1) = chains[2 * h], chains[2 * h + 1]
            later = jnp.concatenate([row_sums(logs[2 * h][1]), row_sums(logs[2 * h + 1][1])], axis=0)
            later_scr[h] = later
            acc_scr[h] = jnp.concatenate(
                [_dot(weights[2 * h], v_ref[0, pl.ds(diag, w0), lanes[h]]),
                 _dot(weights[2 * h + 1], v_ref[0, pl.ds(diag, w1), lanes[h]])], axis=0)
            top = jnp.max(later)
            highest = top if highest is None else jnp.maximum(highest, top)
        return highest

    def earlier_block(start):
        zs = [_dot_nt(qs[h], k_ref[0, pl.ds(start, t), lanes[h]]) for h in heads]
        logs = [gate_logs(z) for z in zs]
        sums = [_dot(log_rest.astype(BF16), suffix_and_total) for _, log_rest in logs]
        highest = None
        weights = []
        for h in heads:
            later = later_scr[h]
            weights.append(jnp.exp2(logs[h][0] + sums[h][:, :t] + _widen(later, t)).astype(BF16))
            later = later + sums[h][:, t:]
            later_scr[h] = later
            top = jnp.max(later)
            highest = top if highest is None else jnp.maximum(highest, top)
        for h in heads:
            acc_scr[h] += _dot(weights[h], v_ref[0, pl.ds(start, t), lanes[h]])
        return highest

    diag = pl.multiple_of(qi * t, t)
    highest_later = diagonal_block()

    def more(c):
        blocks_done, highest = c
        return jnp.logical_and(blocks_done < qi, highest > SB_SKIP_BELOW)

    def one_block(c):
        blocks_done, _ = c
        return blocks_done + 1, earlier_block(pl.multiple_of(diag - (blocks_done + 1) * t, t))

    lax.while_loop(more, one_block, (jnp.int32(0), highest_later))

    for b in range(SB_HEADS_PER_STEP // HEADS_PER_BLOCK):
        o_ref[0, :, b * LANES:(b + 1) * LANES] = jnp.where(
            first, acc_scr[HEADS_PER_BLOCK * b], acc_scr[HEADS_PER_BLOCK * b + 1]).astype(BF16)


def _sb_attention(q, k, v, batch, seq):
    q = q.reshape(batch, seq, MIX_WIDTH)
    k = k.reshape(batch, seq, MIX_WIDTH)
    v = v.reshape(batch, seq, MIX_WIDTH)
    width = SB_HEADS_PER_STEP * HEAD_DIM
    whole = pl.BlockSpec((1, seq, width), lambda b, g, i: (b, 0, g))
    tile = pl.BlockSpec((1, ATTN_TQ, width), lambda b, g, i: (b, i, g))
    o = pl.pallas_call(
        _sb_attn_kernel,
        grid=(batch, N_HEADS // SB_HEADS_PER_STEP, seq // ATTN_TQ),
        in_specs=[tile, whole, whole],
        out_specs=tile,
        out_shape=jax.ShapeDtypeStruct((batch, seq, MIX_WIDTH), BF16),
        scratch_shapes=[pltpu.VMEM((SB_HEADS_PER_STEP, ATTN_TQ, LANES), F32)] * 2,
        compiler_params=_attn_params(),
        name="sb_attention",
    )(q, k, v)
    return o.reshape(batch * seq, MIX_WIDTH)


def _post_kernel(o_ref, gate_ref, x_ref, p_ref, wout_ref, wg_ref, wp_ref, *rest, with_kv):
    groups = [slice(r, r + POST_ROWS) for r in range(0, TOKEN_TILE, POST_ROWS)]
    us = []
    for rows in groups:
        gate = gate_ref[rows, :].astype(F32)
        us.append((o_ref[rows, :].astype(F32) * (gate * _sigmoid(gate))).astype(BF16))
    ples = [_dot(p_ref[rows, :].astype(BF16), wp_ref[...]) for rows in groups]
    ys = [x_ref[rows, :] + _dot(u, wout_ref[...]) for rows, u in zip(groups, us)]
    gates = [_dot(y.astype(BF16), wg_ref[...]) for y in ys]
    x_news = [y + _sigmoid(g) * ple for y, g, ple in zip(ys, gates, ples)]
    if with_kv:
        kvg_ref, wkv_ref, xo_ref, k_ref, v_ref = rest
        kvs = [_dot(_rms(x_new, kvg_ref[...]).astype(BF16), wkv_ref[...]) for x_new in x_news]
        for rows, kv in zip(groups, kvs):
            k_ref[rows, :] = kv[:, :MIX_WIDTH].astype(BF16)
            v_ref[rows, :] = kv[:, MIX_WIDTH:].astype(BF16)
    else:
        (xo_ref,) = rest
    for rows, x_new in zip(groups, x_news):
        xo_ref[rows, :] = x_new


def _layer_spec(rows, cols, layer, col_block=0):
    return pl.BlockSpec((None, rows, cols), lambda i: (layer, 0, col_block))


def _post(o, gate, x, p, layer, w_out, out_layer, w_gate, w_proj, kv_ln_g=None, w_kv=None):
    tokens = x.shape[0]
    with_kv = w_kv is not None
    p_spec = pl.BlockSpec((None, TOKEN_TILE, PLE_DIM), lambda i: (layer, i, 0))
    in_specs = [_row_spec(MIX_WIDTH), _row_spec(MIX_WIDTH), _row_spec(D_MODEL), p_spec,
                _layer_spec(MIX_WIDTH, D_MODEL, out_layer), _layer_spec(D_MODEL, D_MODEL, layer),
                _layer_spec(PLE_DIM, D_MODEL, layer)]
    args = [o, gate, x, p, w_out, w_gate, w_proj]
    out_specs = [_row_spec(D_MODEL)]
    out_shape = [jax.ShapeDtypeStruct((tokens, D_MODEL), F32)]
    if with_kv:
        in_specs += [_const_spec((1, D_MODEL)), _const_spec((D_MODEL, 2 * MIX_WIDTH))]
        args += [kv_ln_g.reshape(1, D_MODEL), w_kv.astype(BF16)]
        out_specs += [_row_spec(MIX_WIDTH), _row_spec(MIX_WIDTH)]
        out_shape += [jax.ShapeDtypeStruct((tokens, MIX_WIDTH), BF16)] * 2
    return pl.pallas_call(
        functools.partial(_post_kernel, with_kv=with_kv),
        grid=(tokens // TOKEN_TILE,),
        in_specs=in_specs, out_specs=out_specs, out_shape=out_shape,
        compiler_params=_params(),
        name="layer_tail_kv" if with_kv else "layer_tail",
    )(*args)


def _sb_front_kernel(x_ref, ln_ref, wq_ref, wgate_ref, q_ref, gate_ref):
    hb = _rms(x_ref[...], ln_ref[...]).astype(BF16)
    q_ref[...] = (_dot(hb, wq_ref[...]) * (HEAD_DIM ** -0.5 * LOG2_E)).astype(BF16)
    gate_ref[...] = _dot(hb, wgate_ref[...]).astype(BF16)


def _sb_front(x, ln_g, w_in, layer):
    tokens = x.shape[0]
    out = jax.ShapeDtypeStruct((tokens, MIX_WIDTH), BF16)
    return pl.pallas_call(
        _sb_front_kernel,
        grid=(tokens // TOKEN_TILE,),
        in_specs=[_row_spec(D_MODEL), _const_spec((1, D_MODEL)),
                  _layer_spec(D_MODEL, MIX_WIDTH, layer, 0), _layer_spec(D_MODEL, MIX_WIDTH, layer, 1)],
        out_specs=[_row_spec(MIX_WIDTH), _row_spec(MIX_WIDTH)],
        out_shape=[out, out],
        compiler_params=_params(),
        name="sb_front",
    )(x, ln_g.reshape(1, D_MODEL), w_in, w_in)


def kernel(x, p, positions, mla_ln_g, mla_w_in, mla_q_norm_g, mla_kv_norm_g, mla_w_q_up, mla_w_kv_up, mla_q_head_g, mla_k_head_g, mla_w_out, kv_ln_g, w_kv_shared, sb_ln_g, sb_w_in, sb_w_out, ple_w_proj, ple_w_gate):
    batch, seq, _ = x.shape
    tokens = batch * seq
    x = x.reshape(tokens, D_MODEL)
    p = p.reshape(DEPTH, tokens, PLE_DIM)
    cos_t, sin_t = _rope_tables(positions)
    mla_w_out, sb_w_out, sb_w_in = mla_w_out.astype(BF16), sb_w_out.astype(BF16), sb_w_in.astype(BF16)
    ple_w_gate, ple_w_proj = ple_w_gate.astype(BF16), ple_w_proj.astype(BF16)
    k_sh = v_sh = None
    for i in range(DEPTH):
        if i < N_A:
            q, k, v, gate = _mla_front(x, mla_ln_g[i], mla_w_in[i], mla_q_norm_g[i], mla_kv_norm_g[i],
                                       mla_w_q_up[i], mla_w_kv_up[i], mla_q_head_g[i], mla_k_head_g[i],
                                       cos_t, sin_t)
            o = _mla_attention(q, k, v, batch, seq)
            w_out, out_layer = mla_w_out, i
        else:
            q, gate = _sb_front(x, sb_ln_g[i - N_A], sb_w_in, i - N_A)
            o = _sb_attention(q, k_sh, v_sh, batch, seq)
            w_out, out_layer = sb_w_out, i - N_A
        if i == N_A - 1:
            x, k_sh, v_sh = _post(o, gate, x, p, i, w_out, out_layer, ple_w_gate, ple_w_proj, kv_ln_g, w_kv_shared)
        else:
            (x,) = _post(o, gate, x, p, i, w_out, out_layer, ple_w_gate, ple_w_proj)
    return x.reshape(batch, seq, D_MODEL)
```

```python
import functools

import jax
import jax.numpy as jnp
from jax import lax
from jax.experimental import pallas as pl
from jax.experimental.pallas import tpu as pltpu

D_MODEL = 1024
DEPTH = 4
N_A = DEPTH // 2
PLE_DIM = 256
N_HEADS = 16
HEAD_DIM = 64
ROPE_DIM = 32
ROPE_HALF = ROPE_DIM // 2
QK_DIM = HEAD_DIM + ROPE_DIM
Q_LORA = 384
KV_LORA = 256
MIX_WIDTH = N_HEADS * HEAD_DIM
ROPE_THETA = 10000.0
EPS = 1e-6

LANES = 128
HEADS_PER_BLOCK = LANES // HEAD_DIM
N_PAIRS = N_HEADS // HEADS_PER_BLOCK
MLA_PAD = N_HEADS * LANES
TOKEN_TILE = 512
FRONT_ROWS = 256
POST_ROWS = 256
ATTN_TQ = 256
KEY_BLOCK = 512
MLA_HEADS_PER_STEP = 16
SB_HEADS_PER_STEP = 16
SB_SKIP_BELOW = -200.0
LOG2_E = 1.4426950408889634
VMEM_LIMIT = 56 * 1024 * 1024

F32 = jnp.float32
BF16 = jnp.bfloat16


def _dot(a, b):
    return jnp.dot(a, b, preferred_element_type=F32)


def _dot_nt(a, b):
    return lax.dot_general(a, b, (((1,), (1,)), ((), ())), preferred_element_type=F32)


def _rms(x, g):
    return x * lax.rsqrt(jnp.mean(x * x, axis=-1, keepdims=True) + EPS) * g


def _sigmoid(x):
    return 1.0 / (1.0 + jnp.exp(-x))


def _params():
    return pltpu.CompilerParams(dimension_semantics=("arbitrary",), vmem_limit_bytes=VMEM_LIMIT)


def _const_spec(shape):
    return pl.BlockSpec(shape, lambda i: (0,) * len(shape))


def _row_spec(width, tile=TOKEN_TILE):
    return pl.BlockSpec((tile, width), lambda i: (i, 0))


def _rope_table_kernel(pos_ref, inv_ref, sign_ref, keep_ref, cos_ref, sin_ref):
    ang = pos_ref[...] * inv_ref[...]
    cos_ref[...] = jnp.cos(ang) * keep_ref[...]
    sin_ref[...] = jnp.sin(ang) * sign_ref[...]


def _head_block_row(nope, first, second):
    pad = jnp.zeros((LANES - QK_DIM - ROPE_HALF,), F32)
    return jnp.concatenate([nope, first, second, first, pad]).reshape(1, LANES)


def _rope_tables(positions):
    tokens = positions.size
    pos = positions.astype(F32).reshape(tokens, 1)
    inv = 1.0 / (ROPE_THETA ** (jnp.arange(ROPE_HALF, dtype=F32) / ROPE_HALF))
    zeros = jnp.zeros((HEAD_DIM,), F32)
    ones = jnp.ones((ROPE_HALF,), F32)
    inv_row = _head_block_row(zeros, inv, inv)
    sign_row = _head_block_row(zeros, -ones, ones) * _keep_row()
    out = jax.ShapeDtypeStruct((tokens, LANES), F32)
    return pl.pallas_call(
        _rope_table_kernel,
        grid=(tokens // TOKEN_TILE,),
        in_specs=[_row_spec(1), _const_spec((1, LANES)), _const_spec((1, LANES)), _const_spec((1, LANES))],
        out_specs=[_row_spec(LANES), _row_spec(LANES)],
        out_shape=[out, out],
        compiler_params=_params(),
        name="rope_tables",
    )(pos, inv_row, sign_row, _keep_row())


def _keep_row():
    return (jnp.arange(LANES) < QK_DIM).astype(F32).reshape(1, LANES)


def _mla_front_kernel(x_ref, ln_ref, wcq_ref, wckv_ref, wgate_ref, qn_ref, kvn_ref,
                      wq_ref, wk_ref, wv_ref, vone_ref, gq_ref, gk_ref, keep_ref, cos_ref, sin_ref,
                      q_ref, k_ref, v_ref, gate_ref):
    keep = keep_ref[...]
    gq = gq_ref[...]
    gk = gk_ref[...]
    pair_cols = [slice(p * 2 * LANES, (p + 1) * 2 * LANES) for p in range(N_PAIRS)]
    head_blocks = [(p, slice(j * LANES, (j + 1) * LANES))
                   for p in range(N_PAIRS) for j in range(HEADS_PER_BLOCK)]
    ones_row = lax.broadcasted_iota(jnp.int32, (2 * LANES, 2 * LANES), 0)
    ones_col = lax.broadcasted_iota(jnp.int32, (2 * LANES, 2 * LANES), 1)
    same_block = ones_row // LANES == ones_col // LANES
    block_ones_k = jnp.where(same_block, 1.0, 0.0).astype(BF16)
    real_lane = jnp.bitwise_and(ones_row, LANES - 1) < QK_DIM
    block_ones_q = jnp.where(jnp.logical_and(same_block, real_lane), 1.0, 0.0).astype(BF16)

    def project(rows):
        hb = _rms(x_ref[rows, :], ln_ref[...]).astype(BF16)
        cq_kr = _dot(hb, wcq_ref[...])
        cq = cq_kr[:, :Q_LORA]
        kr = cq_kr[:, Q_LORA:]
        ckv = _dot(hb, wckv_ref[...])
        gate_ref[rows, :] = _dot(hb, wgate_ref[...]).astype(BF16)
        cqn = _rms(cq, qn_ref[...]).astype(BF16)
        ckvn = _rms(ckv, kvn_ref[...]).astype(BF16)
        v_ref[rows, :] = (_dot(ckvn, wv_ref[...]) + vone_ref[...]).astype(BF16)
        kn2 = [_dot(ckvn, wk_ref[:, c]) for c in pair_cols]
        qh2 = [_dot(cqn, wq_ref[:, c]) for c in pair_cols]
        return kr, kn2, qh2

    def finish(rows, kr, kn2, qh2):
        cos = cos_ref[rows, :]
        sin = sin_ref[rows, :]

        def rope(y):
            return y * cos + pltpu.roll(y, LANES - ROPE_HALF, 1) * sin

        kns = [kn2[p][:, blk] for p, blk in head_blocks]
        qhs = [qh2[p][:, blk] for p, blk in head_blocks]
        ss_kr = jnp.sum(kr * kr * keep, axis=-1, keepdims=True) + QK_DIM * EPS
        ss_k2 = [_dot((kn * kn).astype(BF16), block_ones_k) for kn in kn2]
        ss_q2 = [_dot((qh * qh).astype(BF16), block_ones_q) for qh in qh2]
        r_k = [lax.rsqrt(ss_k2[p][:, blk] + ss_kr) for p, blk in head_blocks]
        r_q = [lax.rsqrt(ss_q2[p][:, blk] + QK_DIM * EPS) for p, blk in head_blocks]
        k_rope = rope(kr * gk)
        q_rot = [rope(qh * gq) for qh in qhs]
        for h in range(N_HEADS):
            out = slice(h * LANES, (h + 1) * LANES)
            k_ref[rows, out] = ((kns[h] * gk + k_rope) * r_k[h]).astype(BF16)
            q_ref[rows, out] = (q_rot[h] * r_q[h]).astype(BF16)

    groups = [slice(r, r + FRONT_ROWS) for r in range(0, TOKEN_TILE, FRONT_ROWS)]
    projected = [project(rows) for rows in groups]
    for rows, raw in zip(groups, projected):
        finish(rows, *raw)


def _rotary_block(w):
    first = w[..., HEAD_DIM:HEAD_DIM + ROPE_HALF]
    pad = jnp.zeros(w.shape[:-1] + (LANES - QK_DIM - ROPE_HALF,), w.dtype)
    return jnp.concatenate([w, first, pad], axis=-1)


def _pad_heads(w, width):
    k = w.shape[0]
    w = w.reshape(k, N_HEADS, width)
    w = jnp.pad(w, ((0, 0), (0, 0), (0, LANES - width)))
    return w.reshape(k, MLA_PAD)


def _mla_front(x, ln_g, w_in, qn_g, kvn_g, w_q_up, w_kv_up, q_head_g, k_head_g, cos_t, sin_t):
    tokens = x.shape[0]
    wckv = w_in[:, Q_LORA:Q_LORA + KV_LORA].astype(BF16)
    wkr = w_in[:, Q_LORA + KV_LORA:Q_LORA + KV_LORA + ROPE_DIM]
    wkr = _rotary_block(jnp.pad(wkr, ((0, 0), (HEAD_DIM, 0))))
    wcq = jnp.concatenate([w_in[:, :Q_LORA], wkr], axis=1).astype(BF16)
    wgate = w_in[:, Q_LORA + KV_LORA + ROPE_DIM:].astype(BF16)
    wq = _rotary_block(w_q_up.reshape(Q_LORA, N_HEADS, QK_DIM)).reshape(Q_LORA, MLA_PAD).astype(BF16)
    wkv = w_kv_up.reshape(KV_LORA, N_HEADS, 2 * HEAD_DIM)
    wk = _pad_heads(wkv[:, :, :HEAD_DIM].reshape(KV_LORA, MIX_WIDTH), HEAD_DIM).astype(BF16)
    wvh = wkv[:, :, HEAD_DIM:]
    odd_head = (jnp.arange(N_HEADS) % HEADS_PER_BLOCK == 1)[None, :, None]
    wv = jnp.where(odd_head, jnp.concatenate([jnp.zeros_like(wvh), wvh], axis=-1),
                   jnp.concatenate([wvh, jnp.zeros_like(wvh)], axis=-1)).reshape(KV_LORA, MLA_PAD).astype(BF16)
    value_lane = (jnp.arange(LANES) < HEAD_DIM)[None, :] != odd_head[0]
    v_one = jnp.where(value_lane, 0.0, 1.0).astype(F32).reshape(1, MLA_PAD)
    k_fold = QK_DIM ** 0.5
    q_fold = LOG2_E
    out = lambda w: jax.ShapeDtypeStruct((tokens, w), BF16)
    return pl.pallas_call(
        _mla_front_kernel,
        grid=(tokens // TOKEN_TILE,),
        in_specs=[_row_spec(D_MODEL), _const_spec((1, D_MODEL)),
                  _const_spec((D_MODEL, Q_LORA + LANES)), _const_spec((D_MODEL, KV_LORA)),
                  _const_spec((D_MODEL, MIX_WIDTH)),
                  _const_spec((1, Q_LORA)), _const_spec((1, KV_LORA)),
                  _const_spec((Q_LORA, MLA_PAD)), _const_spec((KV_LORA, MLA_PAD)),
                  _const_spec((KV_LORA, MLA_PAD)), _const_spec((1, MLA_PAD)),
                  _const_spec((1, LANES)), _const_spec((1, LANES)), _const_spec((1, LANES)),
                  _row_spec(LANES), _row_spec(LANES)],
        out_specs=[_row_spec(MLA_PAD), _row_spec(MLA_PAD), _row_spec(MLA_PAD), _row_spec(MIX_WIDTH)],
        out_shape=[out(MLA_PAD), out(MLA_PAD), out(MLA_PAD), out(MIX_WIDTH)],
        compiler_params=_params(),
        name="mla_front",
    )(x, ln_g.reshape(1, D_MODEL), wcq, wckv, wgate, qn_g.reshape(1, Q_LORA),
      kvn_g.reshape(1, KV_LORA), wq, wk, wv, v_one, _rotary_block(q_head_g * q_fold).reshape(1, LANES),
      _rotary_block(k_head_g * k_fold).reshape(1, LANES), _keep_row(), cos_t, sin_t)


def _attn_params():
    return pltpu.CompilerParams(dimension_semantics=("arbitrary",) * 3, vmem_limit_bytes=VMEM_LIMIT)


def _first_head_lanes(shape):
    return lax.broadcasted_iota(jnp.int32, shape, 1) < HEAD_DIM


def _is_odd(i):
    return jnp.bitwise_and(i, 1) == 1


def _widen(x, width):
    return jnp.concatenate([x] * (width // LANES), axis=1)


def _mla_attn_kernel(q_ref, k_ref, v_ref, o_ref, m_scr, acc_scr):
    qi = pl.program_id(2)
    n_wide = qi // (KEY_BLOCK // ATTN_TQ)
    row = lax.broadcasted_iota(jnp.int32, (ATTN_TQ, ATTN_TQ), 0)
    col = lax.broadcasted_iota(jnp.int32, (ATTN_TQ, ATTN_TQ), 1)
    causal = col <= row
    heads = range(MLA_HEADS_PER_STEP)

    def step(start, width, visible=None, fresh=False):
        scores = [_dot_nt(q_ref[0, :, h * LANES:(h + 1) * LANES],
                          k_ref[0, pl.ds(start, width), h * LANES:(h + 1) * LANES]) for h in heads]
        for h in heads:
            s = scores[h] if visible is None else jnp.where(visible, scores[h], -jnp.inf)
            block_max = jnp.broadcast_to(jnp.max(s, axis=-1, keepdims=True), (ATTN_TQ, LANES))
            m_new = block_max if fresh else jnp.maximum(m_scr[h], block_max)
            p = jnp.exp2(s - _widen(m_new, width))
            alpha = None if fresh else jnp.exp2(m_scr[h] - m_new)
            m_scr[h] = m_new
            pv = _dot(p.astype(BF16), v_ref[0, pl.ds(start, width), h * LANES:(h + 1) * LANES])
            acc_scr[h] = pv if fresh else alpha * acc_scr[h] + pv

    @pl.when(jnp.logical_not(_is_odd(qi)))
    def _():
        step(pl.multiple_of(qi * ATTN_TQ, ATTN_TQ), ATTN_TQ, causal, fresh=True)

    @pl.when(_is_odd(qi))
    def _():
        wide_row = lax.broadcasted_iota(jnp.int32, (ATTN_TQ, KEY_BLOCK), 0)
        wide_col = lax.broadcasted_iota(jnp.int32, (ATTN_TQ, KEY_BLOCK), 1)
        step(pl.multiple_of(n_wide * KEY_BLOCK, KEY_BLOCK), KEY_BLOCK,
             wide_col <= wide_row + (KEY_BLOCK - ATTN_TQ), fresh=True)

    @pl.loop(0, n_wide)
    def _(kb):
        step(pl.multiple_of(kb * KEY_BLOCK, KEY_BLOCK), KEY_BLOCK)

    first = _first_head_lanes((ATTN_TQ, LANES))
    for b in range(MLA_HEADS_PER_STEP // HEADS_PER_BLOCK):
        acc0, acc1 = acc_scr[HEADS_PER_BLOCK * b], acc_scr[HEADS_PER_BLOCK * b + 1]
        numerator = jnp.where(first, acc0, acc1)
        denominator = pltpu.roll(jnp.where(first, acc1, acc0), HEAD_DIM, 1)
        o_ref[0, :, b * LANES:(b + 1) * LANES] = (numerator / denominator).astype(BF16)


def _mla_attention(q, k, v, batch, seq):
    q = q.reshape(batch, seq, MLA_PAD)
    k = k.reshape(batch, seq, MLA_PAD)
    v = v.reshape(batch, seq, MLA_PAD)
    qk_w = MLA_HEADS_PER_STEP * LANES
    v_w = MLA_HEADS_PER_STEP * HEAD_DIM
    o = pl.pallas_call(
        _mla_attn_kernel,
        grid=(batch, N_HEADS // MLA_HEADS_PER_STEP, seq // ATTN_TQ),
        in_specs=[pl.BlockSpec((1, ATTN_TQ, qk_w), lambda b, g, i: (b, i, g)),
                  pl.BlockSpec((1, seq, qk_w), lambda b, g, i: (b, 0, g)),
                  pl.BlockSpec((1, seq, qk_w), lambda b, g, i: (b, 0, g))],
        out_specs=pl.BlockSpec((1, ATTN_TQ, v_w), lambda b, g, i: (b, i, g)),
        out_shape=jax.ShapeDtypeStruct((batch, seq, MIX_WIDTH), BF16),
        scratch_shapes=[pltpu.VMEM((MLA_HEADS_PER_STEP, ATTN_TQ, LANES), F32)] * 2,
        compiler_params=_attn_params(),
        name="mla_attention",
    )(q, k, v)
    return o.reshape(batch * seq, MIX_WIDTH)


def _sb_attn_kernel(q_ref, k_ref, v_ref, o_ref, later_scr, acc_scr):
    qi = pl.program_id(2)
    t = ATTN_TQ
    row = lax.broadcasted_iota(jnp.int32, (t, t), 0)
    col = lax.broadcasted_iota(jnp.int32, (t, t), 1)
    strict = col < row
    suffix = jnp.where(row > col, 1.0, 0.0).astype(BF16)
    suffix_and_total = jnp.concatenate([suffix, jnp.ones((t, LANES), BF16)], axis=1)
    first = _first_head_lanes((t, LANES))
    heads = range(SB_HEADS_PER_STEP)
    lanes = [slice(h // HEADS_PER_BLOCK * LANES, (h // HEADS_PER_BLOCK + 1) * LANES) for h in heads]
    qs = []
    for h in heads:
        q2 = q_ref[0, :, lanes[h]]
        mine = first if h % HEADS_PER_BLOCK == 0 else jnp.logical_not(first)
        qs.append(jnp.where(mine, q2, jnp.zeros_like(q2)))


    def gate_logs(z, visible=None):
        soft = jnp.log(1.0 + jnp.exp2(-jnp.abs(z))) * LOG2_E
        log_beta = jnp.minimum(z, 0.0) - soft
        log_rest = log_beta - z
        if visible is not None:
            log_rest = jnp.where(visible, log_rest, 0.0)
        return log_beta, log_rest

    def row_sums(x):
        return jnp.broadcast_to(jnp.sum(x, axis=-1, keepdims=True), (x.shape[0], LANES))

    def diagonal_block():
        half = t // 2
        parts = [(slice(0, half), half), (slice(half, t), t)]
        chains = [(h, rows, width) for h in heads for rows, width in parts]
        zs = [_dot_nt(qs[h][rows], k_ref[0, pl.ds(diag, width), lanes[h]]) for h, rows, width in chains]
        logs = [gate_logs(z, strict[rows, :width]) for z, (_, rows, width) in zip(zs, chains)]
        sums = [_dot(log_rest.astype(BF16), suffix[:width, :width])
                for (_, log_rest), (_, _, width) in zip(logs, chains)]
        weights = [jnp.where(strict[rows, :width], jnp.exp2(log_beta + s), 0.0).astype(BF16)
                   for (log_beta, _), s, (_, rows, width) in zip(logs, sums, chains)]
        highest = None
        for h in heads:
            (_, _, w0), (_, _, w1) = chains[2 * h], chains[2 * h + 1]
            later = jnp.concatenate([row_sums(logs[2 * h][1]), row_sums(logs[2 * h + 1][1])], axis=0)
            later_scr[h] = later
            acc_scr[h] = jnp.concatenate(
                [_dot(weights[2 * h], v_ref[0, pl.ds(diag, w0), lanes[h]]),
                 _dot(weights[2 * h + 1], v_ref[0, pl.ds(diag, w1), lanes[h]])], axis=0)
            top = jnp.max(later)
            highest = top if highest is None else jnp.maximum(highest, top)
        return highest

    def earlier_block(start):
        zs = [_dot_nt(qs[h], k_ref[0, pl.ds(start, t), lanes[h]]) for h in heads]
        logs = [gate_logs(z) for z in zs]
        sums = [_dot(log_rest.astype(BF16), suffix_and_total) for _, log_rest in logs]
        highest = None
        weights = []
        for h in heads:
            later = later_scr[h]
            weights.append(jnp.exp2(logs[h][0] + sums[h][:, :t] + _widen(later, t)).astype(BF16))
            later = later + sums[h][:, t:]
            later_scr[h] = later
            top = jnp.max(later)
            highest = top if highest is None else jnp.maximum(highest, top)
        for h in heads:
            acc_scr[h] += _dot(weights[h], v_ref[0, pl.ds(start, t), lanes[h]])
        return highest

    diag = pl.multiple_of(qi * t, t)
    highest_later = diagonal_block()

    def more(c):
        blocks_done, highest = c
        return jnp.logical_and(blocks_done < qi, highest > SB_SKIP_BELOW)

    def one_block(c):
        blocks_done, _ = c
        return blocks_done + 1, earlier_block(pl.multiple_of(diag - (blocks_done + 1) * t, t))

    lax.while_loop(more, one_block, (jnp.int32(0), highest_later))

    for b in range(SB_HEADS_PER_STEP // HEADS_PER_BLOCK):
        o_ref[0, :, b * LANES:(b + 1) * LANES] = jnp.where(
            first, acc_scr[HEADS_PER_BLOCK * b], acc_scr[HEADS_PER_BLOCK * b + 1]).astype(BF16)


def _sb_attention(q, k, v, batch, seq):
    q = q.reshape(batch, seq, MIX_WIDTH)
    k = k.reshape(batch, seq, MIX_WIDTH)
    v = v.reshape(batch, seq, MIX_WIDTH)
    width = SB_HEADS_PER_STEP * HEAD_DIM
    whole = pl.BlockSpec((1, seq, width), lambda b, g, i: (b, 0, g))
    tile = pl.BlockSpec((1, ATTN_TQ, width), lambda b, g, i: (b, i, g))
    o = pl.pallas_call(
        _sb_attn_kernel,
        grid=(batch, N_HEADS // SB_HEADS_PER_STEP, seq // ATTN_TQ),
        in_specs=[tile, whole, whole],
        out_specs=tile,
        out_shape=jax.ShapeDtypeStruct((batch, seq, MIX_WIDTH), BF16),
        scratch_shapes=[pltpu.VMEM((SB_HEADS_PER_STEP, ATTN_TQ, LANES), F32)] * 2,
        compiler_params=_attn_params(),
        name="sb_attention",
    )(q, k, v)
    return o.reshape(batch * seq, MIX_WIDTH)


def _post_kernel(o_ref, gate_ref, x_ref, p_ref, wout_ref, wg_ref, wp_ref, *rest, with_kv):
    groups = [slice(r, r + POST_ROWS) for r in range(0, TOKEN_TILE, POST_ROWS)]
    us = []
    for rows in groups:
        gate = gate_ref[rows, :].astype(F32)
        us.append((o_ref[rows, :].astype(F32) * (gate * _sigmoid(gate))).astype(BF16))
    ples = [_dot(p_ref[rows, :].astype(BF16), wp_ref[...]) for rows in groups]
    ys = [x_ref[rows, :] + _dot(u, wout_ref[...]) for rows, u in zip(groups, us)]
    gates = [_dot(y.astype(BF16), wg_ref[...]) for y in ys]
    x_news = [y + _sigmoid(g) * ple for y, g, ple in zip(ys, gates, ples)]
    if with_kv:
        kvg_ref, wkv_ref, xo_ref, k_ref, v_ref = rest
        kvs = [_dot(_rms(x_new, kvg_ref[...]).astype(BF16), wkv_ref[...]) for x_new in x_news]
        for rows, kv in zip(groups, kvs):
            k_ref[rows, :] = kv[:, :MIX_WIDTH].astype(BF16)
            v_ref[rows, :] = kv[:, MIX_WIDTH:].astype(BF16)
    else:
        (xo_ref,) = rest
    for rows, x_new in zip(groups, x_news):
        xo_ref[rows, :] = x_new


def _layer_spec(rows, cols, layer, col_block=0):
    return pl.BlockSpec((None, rows, cols), lambda i: (layer, 0, col_block))


def _post(o, gate, x, p, layer, w_out, out_layer, w_gate, w_proj, kv_ln_g=None, w_kv=None):
    tokens = x.shape[0]
    with_kv = w_kv is not None
    p_spec = pl.BlockSpec((None, TOKEN_TILE, PLE_DIM), lambda i: (layer, i, 0))
    in_specs = [_row_spec(MIX_WIDTH), _row_spec(MIX_WIDTH), _row_spec(D_MODEL), p_spec,
                _layer_spec(MIX_WIDTH, D_MODEL, out_layer), _layer_spec(D_MODEL, D_MODEL, layer),
                _layer_spec(PLE_DIM, D_MODEL, layer)]
    args = [o, gate, x, p, w_out, w_gate, w_proj]
    out_specs = [_row_spec(D_MODEL)]
    out_shape = [jax.ShapeDtypeStruct((tokens, D_MODEL), F32)]
    if with_kv:
        in_specs += [_const_spec((1, D_MODEL)), _const_spec((D_MODEL, 2 * MIX_WIDTH))]
        args += [kv_ln_g.reshape(1, D_MODEL), w_kv.astype(BF16)]
        out_specs += [_row_spec(MIX_WIDTH), _row_spec(MIX_WIDTH)]
        out_shape += [jax.ShapeDtypeStruct((tokens, MIX_WIDTH), BF16)] * 2
    return pl.pallas_call(
        functools.partial(_post_kernel, with_kv=with_kv),
        grid=(tokens // TOKEN_TILE,),
        in_specs=in_specs, out_specs=out_specs, out_shape=out_shape,
        compiler_params=_params(),
        name="layer_tail_kv" if with_kv else "layer_tail",
    )(*args)


def _sb_front_kernel(x_ref, ln_ref, wq_ref, wgate_ref, q_ref, gate_ref):
    hb = _rms(x_ref[...], ln_ref[...]).astype(BF16)
    q_ref[...] = (_dot(hb, wq_ref[...]) * (HEAD_DIM ** -0.5 * LOG2_E)).astype(BF16)
    gate_ref[...] = _dot(hb, wgate_ref[...]).astype(BF16)


def _sb_front(x, ln_g, w_in, layer):
    tokens = x.shape[0]
    out = jax.ShapeDtypeStruct((tokens, MIX_WIDTH), BF16)
    return pl.pallas_call(
        _sb_front_kernel,
        grid=(tokens // TOKEN_TILE,),
        in_specs=[_row_spec(D_MODEL), _const_spec((1, D_MODEL)),
                  _layer_spec(D_MODEL, MIX_WIDTH, layer, 0), _layer_spec(D_MODEL, MIX_WIDTH, layer, 1)],
        out_specs=[_row_spec(MIX_WIDTH), _row_spec(MIX_WIDTH)],
        out_shape=[out, out],
        compiler_params=_params(),
        name="sb_front",
    )(x, ln_g.reshape(1, D_MODEL), w_in, w_in)


def kernel(x, p, positions, mla_ln_g, mla_w_in, mla_q_norm_g, mla_kv_norm_g, mla_w_q_up, mla_w_kv_up, mla_q_head_g, mla_k_head_g, mla_w_out, kv_ln_g, w_kv_shared, sb_ln_g, sb_w_in, sb_w_out, ple_w_proj, ple_w_gate):
    batch, seq, _ = x.shape
    tokens = batch * seq
    x = x.reshape(tokens, D_MODEL)
    p = p.reshape(DEPTH, tokens, PLE_DIM)
    cos_t, sin_t = _rope_tables(positions)
    mla_w_out, sb_w_out, sb_w_in = mla_w_out.astype(BF16), sb_w_out.astype(BF16), sb_w_in.astype(BF16)
    ple_w_gate, ple_w_proj = ple_w_gate.astype(BF16), ple_w_proj.astype(BF16)
    k_sh = v_sh = None
    for i in range(DEPTH):
        if i < N_A:
            q, k, v, gate = _mla_front(x, mla_ln_g[i], mla_w_in[i], mla_q_norm_g[i], mla_kv_norm_g[i],
                                       mla_w_q_up[i], mla_w_kv_up[i], mla_q_head_g[i], mla_k_head_g[i],
                                       cos_t, sin_t)
            o = _mla_attention(q, k, v, batch, seq)
            w_out, out_layer = mla_w_out, i
        else:
            q, gate = _sb_front(x, sb_ln_g[i - N_A], sb_w_in, i - N_A)
            o = _sb_attention(q, k_sh, v_sh, batch, seq)
            w_out, out_layer = sb_w_out, i - N_A
        if i == N_A - 1:
            x, k_sh, v_sh = _post(o, gate, x, p, i, w_out, out_layer, ple_w_gate, ple_w_proj, kv_ln_g, w_kv_shared)
        else:
            (x,) = _post(o, gate, x, p, i, w_out, out_layer, ple_w_gate, ple_w_proj)
    return x.reshape(batch, seq, D_MODEL)
```

```python
import functools

import jax
import jax.numpy as jnp
from jax import lax
from jax.experimental import pallas as pl
from jax.experimental.pallas import tpu as pltpu

D_MODEL = 1024
DEPTH = 4
N_A = DEPTH // 2
PLE_DIM = 256
N_HEADS = 16
HEAD_DIM = 64
ROPE_DIM = 32
ROPE_HALF = ROPE_DIM // 2
QK_DIM = HEAD_DIM + ROPE_DIM
Q_LORA = 384
KV_LORA = 256
MIX_WIDTH = N_HEADS * HEAD_DIM
ROPE_THETA = 10000.0
EPS = 1e-6

LANES = 128
HEADS_PER_BLOCK = LANES // HEAD_DIM
N_PAIRS = N_HEADS // HEADS_PER_BLOCK
MLA_PAD = N_HEADS * LANES
TOKEN_TILE = 512
FRONT_ROWS = 256
POST_ROWS = 256
ATTN_TQ = 256
KEY_BLOCK = 512
MLA_HEADS_PER_STEP = 16
SB_HEADS_PER_STEP = 16
SB_SKIP_BELOW = -200.0
LOG2_E = 1.4426950408889634
VMEM_LIMIT = 56 * 1024 * 1024

F32 = jnp.float32
BF16 = jnp.bfloat16


def _dot(a, b):
    return jnp.dot(a, b, preferred_element_type=F32)


def _dot_nt(a, b):
    return lax.dot_general(a, b, (((1,), (1,)), ((), ())), preferred_element_type=F32)


def _rms(x, g):
    return x * lax.rsqrt(jnp.mean(x * x, axis=-1, keepdims=True) + EPS) * g


def _sigmoid(x):
    return 1.0 / (1.0 + jnp.exp(-x))


def _params():
    return pltpu.CompilerParams(dimension_semantics=("arbitrary",), vmem_limit_bytes=VMEM_LIMIT)


def _const_spec(shape):
    return pl.BlockSpec(shape, lambda i: (0,) * len(shape))


def _row_spec(width, tile=TOKEN_TILE):
    return pl.BlockSpec((tile, width), lambda i: (i, 0))


def _rope_table_kernel(pos_ref, inv_ref, sign_ref, keep_ref, cos_ref, sin_ref):
    ang = pos_ref[...] * inv_ref[...]
    cos_ref[...] = jnp.cos(ang) * keep_ref[...]
    sin_ref[...] = jnp.sin(ang) * sign_ref[...]


def _head_block_row(nope, first, second):
    pad = jnp.zeros((LANES - QK_DIM - ROPE_HALF,), F32)
    return jnp.concatenate([nope, first, second, first, pad]).reshape(1, LANES)


def _rope_tables(positions):
    tokens = positions.size
    pos = positions.astype(F32).reshape(tokens, 1)
    inv = 1.0 / (ROPE_THETA ** (jnp.arange(ROPE_HALF, dtype=F32) / ROPE_HALF))
    zeros = jnp.zeros((HEAD_DIM,), F32)
    ones = jnp.ones((ROPE_HALF,), F32)
    inv_row = _head_block_row(zeros, inv, inv)
    sign_row = _head_block_row(zeros, -ones, ones) * _keep_row()
    out = jax.ShapeDtypeStruct((tokens, LANES), F32)
    return pl.pallas_call(
        _rope_table_kernel,
        grid=(tokens // TOKEN_TILE,),
        in_specs=[_row_spec(1), _const_spec((1, LANES)), _const_spec((1, LANES)), _const_spec((1, LANES))],
        out_specs=[_row_spec(LANES), _row_spec(LANES)],
        out_shape=[out, out],
        compiler_params=_params(),
        name="rope_tables",
    )(pos, inv_row, sign_row, _keep_row())


def _keep_row():
    return (jnp.arange(LANES) < QK_DIM).astype(F32).reshape(1, LANES)


def _mla_front_kernel(x_ref, ln_ref, wcq_ref, wckv_ref, wgate_ref, qn_ref, kvn_ref,
                      wq_ref, wk_ref, wv_ref, vone_ref, gq_ref, gk_ref, keep_ref, cos_ref, sin_ref,
                      q_ref, k_ref, v_ref, gate_ref):
    keep = keep_ref[...]
    gq = gq_ref[...]
    gk = gk_ref[...]
    pair_cols = [slice(p * 2 * LANES, (p + 1) * 2 * LANES) for p in range(N_PAIRS)]
    head_blocks = [(p, slice(j * LANES, (j + 1) * LANES))
                   for p in range(N_PAIRS) for j in range(HEADS_PER_BLOCK)]
    ones_row = lax.broadcasted_iota(jnp.int32, (2 * LANES, 2 * LANES), 0)
    ones_col = lax.broadcasted_iota(jnp.int32, (2 * LANES, 2 * LANES), 1)
    same_block = ones_row // LANES == ones_col // LANES
    block_ones_k = jnp.where(same_block, 1.0, 0.0).astype(BF16)
    real_lane = jnp.bitwise_and(ones_row, LANES - 1) < QK_DIM
    block_ones_q = jnp.where(jnp.logical_and(same_block, real_lane), 1.0, 0.0).astype(BF16)

    def project(rows):
        hb = _rms(x_ref[rows, :], ln_ref[...]).astype(BF16)
        cq_kr = _dot(hb, wcq_ref[...])
        cq = cq_kr[:, :Q_LORA]
        kr = cq_kr[:, Q_LORA:]
        ckv = _dot(hb, wckv_ref[...])
        gate_ref[rows, :] = _dot(hb, wgate_ref[...]).astype(BF16)
        cqn = _rms(cq, qn_ref[...]).astype(BF16)
        ckvn = _rms(ckv, kvn_ref[...]).astype(BF16)
        v_ref[rows, :] = (_dot(ckvn, wv_ref[...]) + vone_ref[...]).astype(BF16)
        kn2 = [_dot(ckvn, wk_ref[:, c]) for c in pair_cols]
        qh2 = [_dot(cqn, wq_ref[:, c]) for c in pair_cols]
        return kr, kn2, qh2

    def finish(rows, kr, kn2, qh2):
        cos = cos_ref[rows, :]
        sin = sin_ref[rows, :]

        def rope(y):
            return y * cos + pltpu.roll(y, LANES - ROPE_HALF, 1) * sin

        kns = [kn2[p][:, blk] for p, blk in head_blocks]
        qhs = [qh2[p][:, blk] for p, blk in head_blocks]
        ss_kr = jnp.sum(kr * kr * keep, axis=-1, keepdims=True) + QK_DIM * EPS
        ss_k2 = [_dot((kn * kn).astype(BF16), block_ones_k) for kn in kn2]
        ss_q2 = [_dot((qh * qh).astype(BF16), block_ones_q) for qh in qh2]
        r_k = [lax.rsqrt(ss_k2[p][:, blk] + ss_kr) for p, blk in head_blocks]
        r_q = [lax.rsqrt(ss_q2[p][:, blk] + QK_DIM * EPS) for p, blk in head_blocks]
        k_rope = rope(kr * gk)
        q_rot = [rope(qh * gq) for qh in qhs]
        for h in range(N_HEADS):
            out = slice(h * LANES, (h + 1) * LANES)
            k_ref[rows, out] = ((kns[h] * gk + k_rope) * r_k[h]).astype(BF16)
            q_ref[rows, out] = (q_rot[h] * r_q[h]).astype(BF16)

    groups = [slice(r, r + FRONT_ROWS) for r in range(0, TOKEN_TILE, FRONT_ROWS)]
    projected = [project(rows) for rows in groups]
    for rows, raw in zip(groups, projected):
        finish(rows, *raw)


def _rotary_block(w):
    first = w[..., HEAD_DIM:HEAD_DIM + ROPE_HALF]
    pad = jnp.zeros(w.shape[:-1] + (LANES - QK_DIM - ROPE_HALF,), w.dtype)
    return jnp.concatenate([w, first, pad], axis=-1)


def _pad_heads(w, width):
    k = w.shape[0]
    w = w.reshape(k, N_HEADS, width)
    w = jnp.pad(w, ((0, 0), (0, 0), (0, LANES - width)))
    return w.reshape(k, MLA_PAD)


def _mla_front(x, ln_g, w_in, qn_g, kvn_g, w_q_up, w_kv_up, q_head_g, k_head_g, cos_t, sin_t):
    tokens = x.shape[0]
    wckv = w_in[:, Q_LORA:Q_LORA + KV_LORA].astype(BF16)
    wkr = w_in[:, Q_LORA + KV_LORA:Q_LORA + KV_LORA + ROPE_DIM]
    wkr = _rotary_block(jnp.pad(wkr, ((0, 0), (HEAD_DIM, 0))))
    wcq = jnp.concatenate([w_in[:, :Q_LORA], wkr], axis=1).astype(BF16)
    wgate = w_in[:, Q_LORA + KV_LORA + ROPE_DIM:].astype(BF16)
    wq = _rotary_block(w_q_up.reshape(Q_LORA, N_HEADS, QK_DIM)).reshape(Q_LORA, MLA_PAD).astype(BF16)
    wkv = w_kv_up.reshape(KV_LORA, N_HEADS, 2 * HEAD_DIM)
    wk = _pad_heads(wkv[:, :, :HEAD_DIM].reshape(KV_LORA, MIX_WIDTH), HEAD_DIM).astype(BF16)
    wvh = wkv[:, :, HEAD_DIM:]
    odd_head = (jnp.arange(N_HEADS) % HEADS_PER_BLOCK == 1)[None, :, None]
    wv = jnp.where(odd_head, jnp.concatenate([jnp.zeros_like(wvh), wvh], axis=-1),
                   jnp.concatenate([wvh, jnp.zeros_like(wvh)], axis=-1)).reshape(KV_LORA, MLA_PAD).astype(BF16)
    value_lane = (jnp.arange(LANES) < HEAD_DIM)[None, :] != odd_head[0]
    v_one = jnp.where(value_lane, 0.0, 1.0).astype(F32).reshape(1, MLA_PAD)
    k_fold = QK_DIM ** 0.5
    q_fold = LOG2_E
    out = lambda w: jax.ShapeDtypeStruct((tokens, w), BF16)
    return pl.pallas_call(
        _mla_front_kernel,
        grid=(tokens // TOKEN_TILE,),
        in_specs=[_row_spec(D_MODEL), _const_spec((1, D_MODEL)),
                  _const_spec((D_MODEL, Q_LORA + LANES)), _const_spec((D_MODEL, KV_LORA)),
                  _const_spec((D_MODEL, MIX_WIDTH)),
                  _const_spec((1, Q_LORA)), _const_spec((1, KV_LORA)),
                  _const_spec((Q_LORA, MLA_PAD)), _const_spec((KV_LORA, MLA_PAD)),
                  _const_spec((KV_LORA, MLA_PAD)), _const_spec((1, MLA_PAD)),
                  _const_spec((1, LANES)), _const_spec((1, LANES)), _const_spec((1, LANES)),
                  _row_spec(LANES), _row_spec(LANES)],
        out_specs=[_row_spec(MLA_PAD), _row_spec(MLA_PAD), _row_spec(MLA_PAD), _row_spec(MIX_WIDTH)],
        out_shape=[out(MLA_PAD), out(MLA_PAD), out(MLA_PAD), out(MIX_WIDTH)],
        compiler_params=_params(),
        name="mla_front",
    )(x, ln_g.reshape(1, D_MODEL), wcq, wckv, wgate, qn_g.reshape(1, Q_LORA),
      kvn_g.reshape(1, KV_LORA), wq, wk, wv, v_one, _rotary_block(q_head_g * q_fold).reshape(1, LANES),
      _rotary_block(k_head_g * k_fold).reshape(1, LANES), _keep_row(), cos_t, sin_t)


def _attn_params():
    return pltpu.CompilerParams(dimension_semantics=("arbitrary",) * 3, vmem_limit_bytes=VMEM_LIMIT)


def _first_head_lanes(shape):
    return lax.broadcasted_iota(jnp.int32, shape, 1) < HEAD_DIM


def _is_odd(i):
    return jnp.bitwise_and(i, 1) == 1


def _widen(x, width):
    return jnp.concatenate([x] * (width // LANES), axis=1)


def _mla_attn_kernel(q_ref, k_ref, v_ref, o_ref, m_scr, acc_scr):
    qi = pl.program_id(2)
    n_wide = qi // (KEY_BLOCK // ATTN_TQ)
    row = lax.broadcasted_iota(jnp.int32, (ATTN_TQ, ATTN_TQ), 0)
    col = lax.broadcasted_iota(jnp.int32, (ATTN_TQ, ATTN_TQ), 1)
    causal = col <= row
    heads = range(MLA_HEADS_PER_STEP)

    def step(start, width, visible=None, fresh=False):
        scores = [_dot_nt(q_ref[0, :, h * LANES:(h + 1) * LANES],
                          k_ref[0, pl.ds(start, width), h * LANES:(h + 1) * LANES]) for h in heads]
        for h in heads:
            s = scores[h] if visible is None else jnp.where(visible, scores[h], -jnp.inf)
            block_max = jnp.broadcast_to(jnp.max(s, axis=-1, keepdims=True), (ATTN_TQ, LANES))
            m_new = block_max if fresh else jnp.maximum(m_scr[h], block_max)
            p = jnp.exp2(s - _widen(m_new, width))
            alpha = None if fresh else jnp.exp2(m_scr[h] - m_new)
            m_scr[h] = m_new
            pv = _dot(p.astype(BF16), v_ref[0, pl.ds(start, width), h * LANES:(h + 1) * LANES])
            acc_scr[h] = pv if fresh else alpha * acc_scr[h] + pv

    @pl.when(jnp.logical_not(_is_odd(qi)))
    def _():
        step(pl.multiple_of(qi * ATTN_TQ, ATTN_TQ), ATTN_TQ, causal, fresh=True)

    @pl.when(_is_odd(qi))
    def _():
        wide_row = lax.broadcasted_iota(jnp.int32, (ATTN_TQ, KEY_BLOCK), 0)
        wide_col = lax.broadcasted_iota(jnp.int32, (ATTN_TQ, KEY_BLOCK), 1)
        step(pl.multiple_of(n_wide * KEY_BLOCK, KEY_BLOCK), KEY_BLOCK,
             wide_col <= wide_row + (KEY_BLOCK - ATTN_TQ), fresh=True)

    @pl.loop(0, n_wide)
    def _(kb):
        step(pl.multiple_of(kb * KEY_BLOCK, KEY_BLOCK), KEY_BLOCK)

    first = _first_head_lanes((ATTN_TQ, LANES))
    for b in range(MLA_HEADS_PER_STEP // HEADS_PER_BLOCK):
        acc0, acc1 = acc_scr[HEADS_PER_BLOCK * b], acc_scr[HEADS_PER_BLOCK * b + 1]
        numerator = jnp.where(first, acc0, acc1)
        denominator = pltpu.roll(jnp.where(first, acc1, acc0), HEAD_DIM, 1)
        o_ref[0, :, b * LANES:(b + 1) * LANES] = (numerator / denominator).astype(BF16)


def _mla_attention(q, k, v, batch, seq):
    q = q.reshape(batch, seq, MLA_PAD)
    k = k.reshape(batch, seq, MLA_PAD)
    v = v.reshape(batch, seq, MLA_PAD)
    qk_w = MLA_HEADS_PER_STEP * LANES
    v_w = MLA_HEADS_PER_STEP * HEAD_DIM
    o = pl.pallas_call(
        _mla_attn_kernel,
        grid=(batch, N_HEADS // MLA_HEADS_PER_STEP, seq // ATTN_TQ),
        in_specs=[pl.BlockSpec((1, ATTN_TQ, qk_w), lambda b, g, i: (b, i, g)),
                  pl.BlockSpec((1, seq, qk_w), lambda b, g, i: (b, 0, g)),
                  pl.BlockSpec((1, seq, qk_w), lambda b, g, i: (b, 0, g))],
        out_specs=pl.BlockSpec((1, ATTN_TQ, v_w), lambda b, g, i: (b, i, g)),
        out_shape=jax.ShapeDtypeStruct((batch, seq, MIX_WIDTH), BF16),
        scratch_shapes=[pltpu.VMEM((MLA_HEADS_PER_STEP, ATTN_TQ, LANES), F32)] * 2,
        compiler_params=_attn_params(),
        name="mla_attention",
    )(q, k, v)
    return o.reshape(batch * seq, MIX_WIDTH)


def _sb_attn_kernel(q_ref, k_ref, v_ref, o_ref, later_scr, acc_scr):
    qi = pl.program_id(2)
    t = ATTN_TQ
    row = lax.broadcasted_iota(jnp.int32, (t, t), 0)
    col = lax.broadcasted_iota(jnp.int32, (t, t), 1)
    strict = col < row
    suffix = jnp.where(row > col, 1.0, 0.0).astype(BF16)
    suffix_and_total = jnp.concatenate([suffix, jnp.ones((t, LANES), BF16)], axis=1)
    first = _first_head_lanes((t, LANES))
    heads = range(SB_HEADS_PER_STEP)
    lanes = [slice(h // HEADS_PER_BLOCK * LANES, (h // HEADS_PER_BLOCK + 1) * LANES) for h in heads]
    qs = []
    for h in heads:
        q2 = q_ref[0, :, lanes[h]]
        mine = first if h % HEADS_PER_BLOCK == 0 else jnp.logical_not(first)
        qs.append(jnp.where(mine, q2, jnp.zeros_like(q2)))


    def gate_logs(z, visible=None):
        soft = jnp.log(1.0 + jnp.exp2(-jnp.abs(z))) * LOG2_E
        log_beta = jnp.minimum(z, 0.0) - soft
        log_rest = log_beta - z
        if visible is not None:
            log_rest = jnp.where(visible, log_rest, 0.0)
        return log_beta, log_rest

    def row_sums(x):
        return jnp.broadcast_to(jnp.sum(x, axis=-1, keepdims=True), (x.shape[0], LANES))

    def diagonal_block():
        half = t // 2
        parts = [(slice(0, half), half), (slice(half, t), t)]
        chains = [(h, rows, width) for h in heads for rows, width in parts]
        zs = [_dot_nt(qs[h][rows], k_ref[0, pl.ds(diag, width), lanes[h]]) for h, rows, width in chains]
        logs = [gate_logs(z, strict[rows, :width]) for z, (_, rows, width) in zip(zs, chains)]
        sums = [_dot(log_rest.astype(BF16), suffix[:width, :width])
                for (_, log_rest), (_, _, width) in zip(logs, chains)]
        weights = [jnp.where(strict[rows, :width], jnp.exp2(log_beta + s), 0.0).astype(BF16)
                   for (log_beta, _), s, (_, rows, width) in zip(logs, sums, chains)]
        highest = None
        for h in heads:
            (_, _, w0), (_, _, w1) = chains[2 * h], chains[2 * h + 1]
            later = jnp.concatenate([row_sums(logs[2 * h][1]), row_sums(logs[2 * h + 1][1])], axis=0)
            later_scr[h] = later
            acc_scr[h] = jnp.concatenate(
                [_dot(weights[2 * h], v_ref[0, pl.ds(diag, w0), lanes[h]]),
                 _dot(weights[2 * h + 1], v_ref[0, pl.ds(diag, w1), lanes[h]])], axis=0)
            top = jnp.max(later)
            highest = top if highest is None else jnp.maximum(highest, top)
        return highest

    def earlier_block(start):
        zs = [_dot_nt(qs[h], k_ref[0, pl.ds(start, t), lanes[h]]) for h in heads]
        logs = [gate_logs(z) for z in zs]
        sums = [_dot(log_rest.astype(BF16), suffix_and_total) for _, log_rest in logs]
        highest = None
        weights = []
        for h in heads:
            later = later_scr[h]
            weights.append(jnp.exp2(logs[h][0] + sums[h][:, :t] + _widen(later, t)).astype(BF16))
            later = later + sums[h][:, t:]
            later_scr[h] = later
            top = jnp.max(later)
            highest = top if highest is None else jnp.maximum(highest, top)
        for h in heads:
            acc_scr[h] += _dot(weights[h], v_ref[0, pl.ds(start, t), lanes[h]])
        return highest

    diag = pl.multiple_of(qi * t, t)
    highest_later = diagonal_block()

    def more(c):
        blocks_done, highest = c
        return jnp.logical_and(blocks_done < qi, highest > SB_SKIP_BELOW)

    def one_block(c):
        blocks_done, _ = c
        return blocks_done + 1, earlier_block(pl.multiple_of(diag - (blocks_done + 1) * t, t))

    lax.while_loop(more, one_block, (jnp.int32(0), highest_later))

    for b in range(SB_HEADS_PER_STEP // HEADS_PER_BLOCK):
        o_ref[0, :, b * LANES:(b + 1) * LANES] = jnp.where(
            first, acc_scr[HEADS_PER_BLOCK * b], acc_scr[HEADS_PER_BLOCK * b + 1]).astype(BF16)


def _sb_attention(q, k, v, batch, seq):
    q = q.reshape(batch, seq, MIX_WIDTH)
    k = k.reshape(batch, seq, MIX_WIDTH)
    v = v.reshape(batch, seq, MIX_WIDTH)
    width = SB_HEADS_PER_STEP * HEAD_DIM
    whole = pl.BlockSpec((1, seq, width), lambda b, g, i: (b, 0, g))
    tile = pl.BlockSpec((1, ATTN_TQ, width), lambda b, g, i: (b, i, g))
    o = pl.pallas_call(
        _sb_attn_kernel,
        grid=(batch, N_HEADS // SB_HEADS_PER_STEP, seq // ATTN_TQ),
        in_specs=[tile, whole, whole],
        out_specs=tile,
        out_shape=jax.ShapeDtypeStruct((batch, seq, MIX_WIDTH), BF16),
        scratch_shapes=[pltpu.VMEM((SB_HEADS_PER_STEP, ATTN_TQ, LANES), F32)] * 2,
        compiler_params=_attn_params(),
        name="sb_attention",
    )(q, k, v)
    return o.reshape(batch * seq, MIX_WIDTH)


def _post_kernel(o_ref, gate_ref, x_ref, p_ref, wout_ref, wg_ref, wp_ref, *rest, with_kv, with_next):
    rest = list(rest)
    kvg_ref, wkv_ref = (rest.pop(0), rest.pop(0)) if with_kv else (None, None)
    nln_ref, nwq_ref, nwg_ref = (rest.pop(0), rest.pop(0), rest.pop(0)) if with_next else (None, None, None)
    xo_ref = rest.pop(0)
    k_ref, v_ref = (rest.pop(0), rest.pop(0)) if with_kv else (None, None)
    nq_ref, ngate_ref = (rest.pop(0), rest.pop(0)) if with_next else (None, None)
    groups = [slice(r, r + POST_ROWS) for r in range(0, TOKEN_TILE, POST_ROWS)]
    us = []
    for rows in groups:
        gate = gate_ref[rows, :].astype(F32)
        us.append((o_ref[rows, :].astype(F32) * (gate * _sigmoid(gate))).astype(BF16))
    ples = [_dot(p_ref[rows, :].astype(BF16), wp_ref[...]) for rows in groups]
    ys = [x_ref[rows, :] + _dot(u, wout_ref[...]) for rows, u in zip(groups, us)]
    gates = [_dot(y.astype(BF16), wg_ref[...]) for y in ys]
    x_news = [y + _sigmoid(g) * ple for y, g, ple in zip(ys, gates, ples)]
    if with_kv:
        kvs = [_dot(_rms(x_new, kvg_ref[...]).astype(BF16), wkv_ref[...]) for x_new in x_news]
        for rows, kv in zip(groups, kvs):
            k_ref[rows, :] = kv[:, :MIX_WIDTH].astype(BF16)
            v_ref[rows, :] = kv[:, MIX_WIDTH:].astype(BF16)
    if with_next:
        hbs = [_rms(x_new, nln_ref[...]).astype(BF16) for x_new in x_news]
        for rows, hb in zip(groups, hbs):
            nq_ref[rows, :] = (_dot(hb, nwq_ref[...]) * (HEAD_DIM ** -0.5 * LOG2_E)).astype(BF16)
            ngate_ref[rows, :] = _dot(hb, nwg_ref[...]).astype(BF16)
    for rows, x_new in zip(groups, x_news):
        xo_ref[rows, :] = x_new


def _layer_spec(rows, cols, layer, col_block=0):
    return pl.BlockSpec((None, rows, cols), lambda i: (layer, 0, col_block))


def _post(o, gate, x, p, layer, w_out, out_layer, w_gate, w_proj, kv_ln_g=None, w_kv=None, next_front=None):
    tokens = x.shape[0]
    with_kv = w_kv is not None
    with_next = next_front is not None
    p_spec = pl.BlockSpec((None, TOKEN_TILE, PLE_DIM), lambda i: (layer, i, 0))
    in_specs = [_row_spec(MIX_WIDTH), _row_spec(MIX_WIDTH), _row_spec(D_MODEL), p_spec,
                _layer_spec(MIX_WIDTH, D_MODEL, out_layer), _layer_spec(D_MODEL, D_MODEL, layer),
                _layer_spec(PLE_DIM, D_MODEL, layer)]
    args = [o, gate, x, p, w_out, w_gate, w_proj]
    out_specs = [_row_spec(D_MODEL)]
    out_shape = [jax.ShapeDtypeStruct((tokens, D_MODEL), F32)]
    if with_kv:
        in_specs += [_const_spec((1, D_MODEL)), _const_spec((D_MODEL, 2 * MIX_WIDTH))]
        args += [kv_ln_g.reshape(1, D_MODEL), w_kv.astype(BF16)]
        out_specs += [_row_spec(MIX_WIDTH), _row_spec(MIX_WIDTH)]
        out_shape += [jax.ShapeDtypeStruct((tokens, MIX_WIDTH), BF16)] * 2
    if with_next:
        next_ln, next_w_in, next_layer = next_front
        in_specs += [_const_spec((1, D_MODEL)), _layer_spec(D_MODEL, MIX_WIDTH, next_layer, 0),
                     _layer_spec(D_MODEL, MIX_WIDTH, next_layer, 1)]
        args += [next_ln.reshape(1, D_MODEL), next_w_in, next_w_in]
        out_specs += [_row_spec(MIX_WIDTH), _row_spec(MIX_WIDTH)]
        out_shape += [jax.ShapeDtypeStruct((tokens, MIX_WIDTH), BF16)] * 2
    return pl.pallas_call(
        functools.partial(_post_kernel, with_kv=with_kv, with_next=with_next),
        grid=(tokens // TOKEN_TILE,),
        in_specs=in_specs, out_specs=out_specs, out_shape=out_shape,
        compiler_params=_params(),
        name=("layer_tail_kv" if with_kv else "layer_tail") + ("_front" if with_next else ""),
    )(*args)


def kernel(x, p, positions, mla_ln_g, mla_w_in, mla_q_norm_g, mla_kv_norm_g, mla_w_q_up, mla_w_kv_up, mla_q_head_g, mla_k_head_g, mla_w_out, kv_ln_g, w_kv_shared, sb_ln_g, sb_w_in, sb_w_out, ple_w_proj, ple_w_gate):
    batch, seq, _ = x.shape
    tokens = batch * seq
    x = x.reshape(tokens, D_MODEL)
    p = p.reshape(DEPTH, tokens, PLE_DIM)
    cos_t, sin_t = _rope_tables(positions)
    mla_w_out, sb_w_out, sb_w_in = mla_w_out.astype(BF16), sb_w_out.astype(BF16), sb_w_in.astype(BF16)
    ple_w_gate, ple_w_proj = ple_w_gate.astype(BF16), ple_w_proj.astype(BF16)
    k_sh = v_sh = None
    for i in range(DEPTH):
        if i < N_A:
            q, k, v, gate = _mla_front(x, mla_ln_g[i], mla_w_in[i], mla_q_norm_g[i], mla_kv_norm_g[i],
                                       mla_w_q_up[i], mla_w_kv_up[i], mla_q_head_g[i], mla_k_head_g[i],
                                       cos_t, sin_t)
            o = _mla_attention(q, k, v, batch, seq)
            w_out, out_layer = mla_w_out, i
        else:
            q, gate = next_q, next_gate
            o = _sb_attention(q, k_sh, v_sh, batch, seq)
            w_out, out_layer = sb_w_out, i - N_A
        feeds_sb = N_A <= i + 1 < DEPTH
        next_front = (sb_ln_g[i + 1 - N_A], sb_w_in, i + 1 - N_A) if feeds_sb else None
        kv_args = (kv_ln_g, w_kv_shared) if i == N_A - 1 else (None, None)
        outs = _post(o, gate, x, p, i, w_out, out_layer, ple_w_gate, ple_w_proj, *kv_args, next_front=next_front)
        x = outs[0]
        if i == N_A - 1:
            k_sh, v_sh = outs[1], outs[2]
        if feeds_sb:
            next_q, next_gate = outs[-2], outs[-1]
    return x.reshape(batch, seq, D_MODEL)
```
